```python
import math
import jax, jax.numpy as jnp
from jax import lax
import numpy as np

D_MODEL = 1024
BATCH = 4
SEQ = 4096
DEPTH = 4

N_MIXERS = 3
N_CONV_LAYERS = (DEPTH + 2) // 3
N_POOL_LAYERS = (DEPTH + 1) // 3
N_ATTN_LAYERS = DEPTH // 3
CONV_WIDTH = 3
POOL_WINDOWS = (2, 4, 8, 16)
N_POOL_GROUPS = len(POOL_WINDOWS)
POOL_GROUP_CH = D_MODEL // N_POOL_GROUPS
HEAD_DIM = 128
N_HEADS = D_MODEL // (2 * HEAD_DIM)
Q_BLOCK = 128
NUM_BUCKETS = 32
MAX_DISTANCE = 128
D_FF = ((8 * D_MODEL // 3 + 255) // 256) * 256
EPS = 1e-6

kernel_name = "hybrid_conv_pool_diffattn_encoder"


def _rms(x, g):
    xf = x.astype(jnp.float32)
    y = xf * lax.rsqrt(jnp.mean(xf * xf, axis=-1, keepdims=True) + EPS)
    return (y * g.astype(jnp.float32)).astype(x.dtype)


def _lambda_init(layer_idx):
    return 0.8 - 0.6 * math.exp(-0.3 * layer_idx)


def _t5_bucket(rel):
    half = NUM_BUCKETS // 2
    ret = jnp.where(rel > 0, half, 0)
    n = jnp.abs(rel)
    max_exact = half // 2
    nf = jnp.maximum(n, 1).astype(jnp.float32)
    large = max_exact + (jnp.log(nf / max_exact) / math.log(MAX_DISTANCE / max_exact)
                         * (half - max_exact)).astype(jnp.int32)
    large = jnp.minimum(large, half - 1)
    return ret + jnp.where(n < max_exact, n, large)


def _conv_mixer(xn, w_in, w_conv, w_out):
    h = xn @ w_in
    gate_b, gate_c, hx = jnp.split(h, 3, axis=-1)
    u = gate_c * hx
    up = jnp.pad(u, ((0, 0), (1, 1), (0, 0)))
    conv = w_conv[0] * up[:, :-2] + w_conv[1] * up[:, 1:-1] + w_conv[2] * up[:, 2:]
    return (gate_b * conv) @ w_out


def _pool_mixer(xn, w_pool, scale):
    bsz, s, _ = xn.shape
    xf = xn.astype(jnp.float32)
    cs = jnp.concatenate([jnp.zeros((bsz, 1, D_MODEL), jnp.float32),
                          jnp.cumsum(xf, axis=1)], axis=1)
    t = jnp.arange(s)
    outs = []
    for g, w in enumerate(POOL_WINDOWS):
        lo = jnp.maximum(t - w // 2, 0)
        hi = jnp.minimum(t + w - 1 - w // 2, s - 1)
        sl = slice(g * POOL_GROUP_CH, (g + 1) * POOL_GROUP_CH)
        csg = cs[..., sl]
        win_sum = jnp.take(csg, hi + 1, axis=1) - jnp.take(csg, lo, axis=1)
        cnt = (hi - lo + 1).astype(jnp.float32)
        outs.append(win_sum / cnt[None, :, None] - xf[..., sl])
    p = jnp.stack(outs, axis=2)
    y = jnp.einsum('bsgc,gcd->bsgd', p, w_pool.astype(jnp.float32)).reshape(bsz, s, D_MODEL)
    return (y * scale.astype(jnp.float32)).astype(xn.dtype)


def _diff_attention(xn, w_qkv, w_o, lq1, lk1, lq2, lk2, subln_g, rel_bias, lambda_init):
    bsz, s, _ = xn.shape
    qkv = xn @ w_qkv
    q, k, v = jnp.split(qkv, 3, axis=-1)
    q = q.reshape(bsz, s, N_HEADS, 2, HEAD_DIM) * (HEAD_DIM ** -0.5)
    k = k.reshape(bsz, s, N_HEADS, 2, HEAD_DIM)
    v = v.reshape(bsz, s, N_HEADS, 2 * HEAD_DIM).transpose(0, 2, 1, 3)
    lam = (jnp.exp(jnp.sum(lq1.astype(jnp.float32) * lk1.astype(jnp.float32)))
           - jnp.exp(jnp.sum(lq2.astype(jnp.float32) * lk2.astype(jnp.float32)))
           + lambda_init)
    nb = s // Q_BLOCK
    qb = q.reshape(bsz, nb, Q_BLOCK, N_HEADS, 2, HEAD_DIM).transpose(1, 0, 3, 4, 2, 5)
    kt = k.transpose(0, 2, 3, 1, 4)
    kpos = jnp.arange(s)

    def block(args):
        qblk, start = args
        qpos = start + jnp.arange(Q_BLOCK)
        bucket = _t5_bucket(kpos[None, :] - qpos[:, None])
        bias = jnp.take(rel_bias, bucket, axis=0).transpose(2, 0, 1).astype(jnp.float32)
        logits = jnp.einsum('bhtqd,bhtkd->bhtqk', qblk, kt).astype(jnp.float32) + bias[None, :, None]
        p = jax.nn.softmax(logits, axis=-1)
        a = p[:, :, 0] - lam * p[:, :, 1]
        return jnp.einsum('bhqk,bhke->bhqe', a.astype(v.dtype), v)

    starts = jnp.arange(nb) * Q_BLOCK
    o = lax.map(block, (qb, starts))
    o = o.transpose(1, 0, 3, 2, 4).reshape(bsz, s, N_HEADS, 2 * HEAD_DIM)
    o = _rms(o, subln_g) * (1.0 - lambda_init)
    return o.reshape(bsz, s, D_MODEL) @ w_o


def _swiglu(xn, w_gate, w_up, w_down):
    return (jax.nn.silu(xn @ w_gate) * (xn @ w_up)) @ w_down


def setup_inputs(seed: int = 0) -> dict:
    key = jax.random.key(seed)
    ks = jax.random.split(key, 18)
    n = jax.random.normal
    f32 = jnp.float32
    sd = D_MODEL ** -0.5
    return {
        "x": n(ks[0], (BATCH, SEQ, D_MODEL), f32),
        "norm_g": 1.0 + 0.1 * n(ks[1], (DEPTH, 4, D_MODEL), f32),
        "conv_w_in": sd * n(ks[2], (N_CONV_LAYERS, D_MODEL, 3 * D_MODEL), f32),
        "conv_w": (CONV_WIDTH ** -0.5) * n(ks[3], (N_CONV_LAYERS, CONV_WIDTH, D_MODEL), f32),
        "conv_w_out": sd * n(ks[4], (N_CONV_LAYERS, D_MODEL, D_MODEL), f32),
        "pool_w": (POOL_GROUP_CH ** -0.5) * n(ks[5], (N_POOL_LAYERS, N_POOL_GROUPS, POOL_GROUP_CH, POOL_GROUP_CH), f32),
        "pool_scale": 1.0 + 0.1 * n(ks[6], (N_POOL_LAYERS, D_MODEL), f32),
        "attn_w_qkv": sd * n(ks[7], (N_ATTN_LAYERS, D_MODEL, 3 * D_MODEL), f32),
        "attn_w_o": sd * n(ks[8], (N_ATTN_LAYERS, D_MODEL, D_MODEL), f32),
        "lambda_q1": 0.1 * n(ks[9], (N_ATTN_LAYERS, HEAD_DIM), f32),
        "lambda_k1": 0.1 * n(ks[10], (N_ATTN_LAYERS, HEAD_DIM), f32),
        "lambda_q2": 0.1 * n(ks[11], (N_ATTN_LAYERS, HEAD_DIM), f32),
        "lambda_k2": 0.1 * n(ks[12], (N_ATTN_LAYERS, HEAD_DIM), f32),
        "attn_subln_g": 1.0 + 0.1 * n(ks[13], (N_ATTN_LAYERS, 2 * HEAD_DIM), f32),
        "rel_bias": 0.5 * n(ks[14], (NUM_BUCKETS, N_HEADS), f32),
        "ffn_w_gate": sd * n(ks[15], (DEPTH, D_MODEL, D_FF), f32),
        "ffn_w_up": sd * n(ks[16], (DEPTH, D_MODEL, D_FF), f32),
        "ffn_w_down": (D_FF ** -0.5) * n(ks[17], (DEPTH, D_FF, D_MODEL), f32),
    }


def reference(x, norm_g, conv_w_in, conv_w, conv_w_out, pool_w, pool_scale,
              attn_w_qkv, attn_w_o, lambda_q1, lambda_k1, lambda_q2, lambda_k2,
              attn_subln_g, rel_bias, ffn_w_gate, ffn_w_up, ffn_w_down):
    ia, ib, ic = 0, 0, 0
    for i in range(DEPTH):
        g = norm_g[i]
        hn = _rms(x, g[0])
        kind = i % N_MIXERS
        if kind == 0:
            m = _conv_mixer(hn, conv_w_in[ia], conv_w[ia], conv_w_out[ia])
            ia += 1
        elif kind == 1:
            m = _pool_mixer(hn, pool_w[ib], pool_scale[ib])
            ib += 1
        else:
            m = _diff_attention(hn, attn_w_qkv[ic], attn_w_o[ic], lambda_q1[ic], lambda_k1[ic],
                                lambda_q2[ic], lambda_k2[ic], attn_subln_g[ic], rel_bias,
                                _lambda_init(i))
            ic += 1
        x = x + _rms(m, g[1])
        f = _swiglu(_rms(x, g[2]), ffn_w_gate[i], ffn_w_up[i], ffn_w_down[i])
        x = x + _rms(f, g[3])
    return x
```

```python
import functools
import math

import jax
import jax.numpy as jnp
from jax import lax
from jax.experimental import pallas as pl
from jax.experimental.pallas import tpu as pltpu

D_MODEL = 1024
DEPTH = 4
N_MIXERS = 3
CONV_WIDTH = 3
POOL_WINDOWS = (2, 4, 8, 16)
POOL_GROUP_CH = D_MODEL // len(POOL_WINDOWS)
HEAD_DIM = 128
N_HEADS = D_MODEL // (2 * HEAD_DIM)
NUM_BUCKETS = 32
MAX_DISTANCE = 128
D_FF = ((8 * D_MODEL // 3 + 255) // 256) * 256
EPS = 1e-6

F32 = jnp.float32
BF16 = jnp.bfloat16

SUBLANES = 8
LANES = 128
MXU_DIM = 256
VMEM_BYTES = 64 * 1024 * 1024

TOKEN_TILE = 512
POOL_TILE = 256
ATTN_TILE = 512
FF_CHUNK = MXU_DIM
CONV_CHUNK = MXU_DIM
HALO_PAD = LANES


def _lambda_init(layer_idx):
    return 0.8 - 0.6 * math.exp(-0.3 * layer_idx)


def _vmem_limit(nbytes):
    return int(min(VMEM_BYTES - (4 << 20), nbytes))


def _rms(x, g):
    ms = jnp.mean(x * x, axis=-1, keepdims=True)
    return (x * lax.rsqrt(ms + EPS)) * g


def _dot(a, b):
    return jnp.dot(a, b, preferred_element_type=F32)


def _const_spec(shape):
    return pl.BlockSpec(shape, lambda *_: (0,) * len(shape), pipeline_mode=pl.Buffered(1))


def _ffn_kernel(x_ref, g_ref, wg_ref, wu_ref, wd_ref, o_ref, h_scr):
    x = x_ref[...]
    xn = _rms(x, g_ref[2:3, :]).astype(BF16)
    for c in range(0, D_FF, FF_CHUNK):
        gate = _dot(xn, wg_ref[:, c:c + FF_CHUNK])
        up = _dot(xn, wu_ref[:, c:c + FF_CHUNK])
        h_scr[:, c:c + FF_CHUNK] = (jax.nn.silu(gate) * up).astype(BF16)
    f = _dot(h_scr[...], wd_ref[...])
    o_ref[...] = x + _rms(f, g_ref[3:4, :])


def _ffn(x2, g, wg, wu, wd):
    n_tok = x2.shape[0]
    tm = TOKEN_TILE
    weights = 3 * D_MODEL * D_FF * 2
    tiles = 4 * tm * D_MODEL * 4 + tm * D_FF * 2
    return pl.pallas_call(
        _ffn_kernel,
        out_shape=jax.ShapeDtypeStruct(x2.shape, F32),
        grid=(n_tok // tm,),
        in_specs=[
            pl.BlockSpec((tm, D_MODEL), lambda i: (i, 0)),
            _const_spec((4, D_MODEL)),
            _const_spec((D_MODEL, D_FF)),
            _const_spec((D_MODEL, D_FF)),
            _const_spec((D_FF, D_MODEL)),
        ],
        out_specs=pl.BlockSpec((tm, D_MODEL), lambda i: (i, 0)),
        scratch_shapes=[pltpu.VMEM((tm, D_FF), BF16)],
        compiler_params=pltpu.CompilerParams(
            dimension_semantics=("parallel",),
            vmem_limit_bytes=_vmem_limit(weights + tiles + (16 << 20))),
        name="ffn",
    )(x2, g, wg, wu, wd)


def _conv_kernel(xp_ref, x_ref, xn_ref, g_ref, win_ref, cw_ref, wout_ref, o_ref, v_scr):
    i = pl.program_id(1)
    last = pl.num_programs(1) - 1
    tm = x_ref.shape[1]
    d = D_MODEL
    x = x_ref[0]
    g0 = g_ref[0:1, :]
    hn = _rms(x, g0).astype(BF16)
    halo = jnp.concatenate([xp_ref[0], xn_ref[0]], axis=0)
    hh = _rms(halo, g0).astype(BF16)
    row = lax.broadcasted_iota(jnp.int32, (tm, 1), 0)
    for c in range(0, d, CONV_CHUNK):
        cs = slice(c, c + CONV_CHUNK)
        gate_b = _dot(hn, win_ref[:, c:c + CONV_CHUNK])
        u = (_dot(hn, win_ref[:, d + c:d + c + CONV_CHUNK])
             * _dot(hn, win_ref[:, 2 * d + c:2 * d + c + CONV_CHUNK]))
        uh = (_dot(hh, win_ref[:, d + c:d + c + CONV_CHUNK])
              * _dot(hh, win_ref[:, 2 * d + c:2 * d + c + CONV_CHUNK]))
        u_prev = jnp.where(i > 0, uh[SUBLANES - 1:SUBLANES, :], 0.0)
        u_next = jnp.where(i < last, uh[SUBLANES:SUBLANES + 1, :], 0.0)
        u_dn = jnp.where(row == 0, u_prev, pltpu.roll(u, 1, axis=0))
        u_up = jnp.where(row == tm - 1, u_next, pltpu.roll(u, tm - 1, axis=0))
        conv = cw_ref[0:1, cs] * u_dn + cw_ref[1:2, cs] * u + cw_ref[2:3, cs] * u_up
        v_scr[:, cs] = (gate_b * conv).astype(BF16)
    y = _dot(v_scr[...], wout_ref[...])
    o_ref[0] = x + _rms(y, g_ref[1:2, :])


def _conv_mixer(x, g, w_in, w_conv, w_out):
    b, s, d = x.shape
    tm = TOKEN_TILE
    nt = s // tm
    hb = tm // SUBLANES
    weights = (3 * d * d + d * d) * 2
    tiles = 4 * tm * d * 4 + tm * d * 2
    return pl.pallas_call(
        _conv_kernel,
        out_shape=jax.ShapeDtypeStruct(x.shape, F32),
        grid=(b, nt),
        in_specs=[
            pl.BlockSpec((1, SUBLANES, d), lambda bi, i: (bi, jnp.maximum(i * hb - 1, 0), 0)),
            pl.BlockSpec((1, tm, d), lambda bi, i: (bi, i, 0)),
            pl.BlockSpec((1, SUBLANES, d),
                         lambda bi, i: (bi, jnp.minimum((i + 1) * hb, s // SUBLANES - 1), 0)),
            _const_spec((4, d)),
            _const_spec((d, 3 * d)),
            _const_spec((CONV_WIDTH, d)),
            _const_spec((d, d)),
        ],
        out_specs=pl.BlockSpec((1, tm, d), lambda bi, i: (bi, i, 0)),
        scratch_shapes=[pltpu.VMEM((tm, d), BF16)],
        compiler_params=pltpu.CompilerParams(
            dimension_semantics=("parallel", "parallel"),
            vmem_limit_bytes=_vmem_limit(weights + tiles + (16 << 20))),
        name="conv_mixer",
    )(x, x, x, g, w_in, w_conv, w_out)


def _pool_kernel(xp_ref, x_ref, xn_ref, g_ref, wp_ref, ps_ref, o_ref, *, seq_len):
    i = pl.program_id(1)
    last = pl.num_programs(1) - 1
    tm = x_ref.shape[1]
    te = tm + HALO_PAD
    x = x_ref[0]
    g0 = g_ref[0:1, :]
    ext = jnp.concatenate(
        [xp_ref[0], x, xn_ref[0], jnp.zeros((HALO_PAD - 2 * SUBLANES, D_MODEL), F32)], axis=0)
    erow = lax.broadcasted_iota(jnp.int32, (te, 1), 0)
    inside = ((erow >= SUBLANES) | (i > 0)) & ((erow < tm + SUBLANES) | (i < last))
    hn_ext = jnp.where(inside, _rms(ext, g0), 0.0)
    hn_ext_b = hn_ext.astype(BF16)
    hn = hn_ext[SUBLANES:SUBLANES + tm, :]
    rel = (lax.broadcasted_iota(jnp.int32, (tm, te), 1) - SUBLANES
           - lax.broadcasted_iota(jnp.int32, (tm, te), 0))
    t = i * tm + lax.broadcasted_iota(jnp.int32, (tm, 1), 0)
    outs = []
    for gi, w in enumerate(POOL_WINDOWS):
        lo_off, hi_off = -(w // 2), w - 1 - w // 2
        cs = slice(gi * POOL_GROUP_CH, (gi + 1) * POOL_GROUP_CH)
        band = jnp.where((rel >= lo_off) & (rel <= hi_off), 1.0, 0.0).astype(BF16)
        win_sum = _dot(band, hn_ext_b[:, cs])
        cnt = (jnp.minimum(t + hi_off, seq_len - 1) - jnp.maximum(t + lo_off, 0) + 1).astype(F32)
        p = win_sum / cnt - hn[:, cs]
        outs.append(_dot(p.astype(BF16), wp_ref[gi]))
    y = jnp.concatenate(outs, axis=1) * ps_ref[...]
    o_ref[0] = x + _rms(y, g_ref[1:2, :])


def _pool_mixer(x, g, w_pool, scale):
    b, s, d = x.shape
    tm = POOL_TILE
    nt = s // tm
    hb = tm // SUBLANES
    ng = len(POOL_WINDOWS)
    return pl.pallas_call(
        functools.partial(_pool_kernel, seq_len=s),
        out_shape=jax.ShapeDtypeStruct(x.shape, F32),
        grid=(b, nt),
        in_specs=[
            pl.BlockSpec((1, SUBLANES, d), lambda bi, i: (bi, jnp.maximum(i * hb - 1, 0), 0)),
            pl.BlockSpec((1, tm, d), lambda bi, i: (bi, i, 0)),
            pl.BlockSpec((1, SUBLANES, d),
                         lambda bi, i: (bi, jnp.minimum((i + 1) * hb, s // SUBLANES - 1), 0)),
            _const_spec((4, d)),
            _const_spec((ng, POOL_GROUP_CH, POOL_GROUP_CH)),
            _const_spec((1, d)),
        ],
        out_specs=pl.BlockSpec((1, tm, d), lambda bi, i: (bi, i, 0)),
        compiler_params=pltpu.CompilerParams(
            dimension_semantics=("parallel", "parallel"),
            vmem_limit_bytes=_vmem_limit(32 << 20)),
        name="pool_mixer",
    )(x, x, x, g, w_pool, scale)


def _qkv_kernel(x_ref, g_ref, w_ref, q_ref, k_ref, v_ref):
    d = D_MODEL
    hn = _rms(x_ref[...], g_ref[0:1, :]).astype(BF16)
    q_ref[...] = (_dot(hn, w_ref[:, :d]) * (HEAD_DIM ** -0.5)).astype(BF16)
    k_ref[...] = _dot(hn, w_ref[:, d:2 * d]).astype(BF16)
    v_ref[...] = _dot(hn, w_ref[:, 2 * d:]).astype(BF16)


def _qkv_proj(x2, g, w_qkv):
    n_tok, d = x2.shape
    tm = TOKEN_TILE
    out = jax.ShapeDtypeStruct((n_tok, d), BF16)
    tile = pl.BlockSpec((tm, d), lambda i: (i, 0))
    return pl.pallas_call(
        _qkv_kernel,
        out_shape=(out, out, out),
        grid=(n_tok // tm,),
        in_specs=[tile, _const_spec((4, d)), _const_spec((d, 3 * d))],
        out_specs=(tile, tile, tile),
        compiler_params=pltpu.CompilerParams(
            dimension_semantics=("parallel",),
            vmem_limit_bytes=_vmem_limit(3 * d * d * 2 + (24 << 20))),
        name="qkv_proj",
    )(x2, g, w_qkv)


def _t5_bucket(rel):
    half = NUM_BUCKETS // 2
    ret = jnp.where(rel > 0, half, 0)
    n = jnp.abs(rel)
    max_exact = half // 2
    nf = jnp.maximum(n, 1).astype(F32)
    large = max_exact + (jnp.log(nf / max_exact) / math.log(MAX_DISTANCE / max_exact)
                         * (half - max_exact)).astype(jnp.int32)
    large = jnp.minimum(large, half - 1)
    return ret + jnp.where(n < max_exact, n, large)


N_BIAS_TILES = 5


def _bias_tile_kernel(rb_ref, bkt_ref, o_ref):
    h = pl.program_id(0)
    t = o_ref.shape[-1]
    bkt = bkt_ref[0]
    val = jnp.zeros(bkt.shape, F32)
    for b in range(NUM_BUCKETS):
        val = jnp.where(bkt == b, rb_ref[b, h], val)
    full = jnp.broadcast_to(val, (t, 2 * t))
    skew = pltpu.roll(full, 0, 1, stride=1, stride_axis=0)
    o_ref[0, 0] = skew[:, :t]


def _bias_tiles(rel_bias, t):
    m = jnp.arange(2 * t, dtype=jnp.int32)
    off = jnp.where(m < t, m, m - 2 * t)
    half = N_BIAS_TILES // 2
    tile_off = jnp.arange(-half, half + 1, dtype=jnp.int32)
    bkt = _t5_bucket(tile_off[:, None] * t + off[None, :]).reshape(N_BIAS_TILES, 1, 2 * t)
    return pl.pallas_call(
        _bias_tile_kernel,
        out_shape=jax.ShapeDtypeStruct((N_HEADS, N_BIAS_TILES, t, t), F32),
        grid=(N_HEADS, N_BIAS_TILES),
        in_specs=[
            pl.BlockSpec(memory_space=pltpu.SMEM),
            pl.BlockSpec((1, 1, 2 * t), lambda h, k: (k, 0, 0)),
        ],
        out_specs=pl.BlockSpec((1, 1, t, t), lambda h, k: (h, k, 0, 0)),
        compiler_params=pltpu.CompilerParams(dimension_semantics=("parallel", "parallel")),
        name="rel_bias_tiles",
    )(rel_bias, bkt)


def _attn_kernel(q_ref, k_ref, v_ref, bias_ref, lam_ref, sg_ref, o_ref,
                 m_scr, l_scr, acc_scr, *, lambda_init):
    i = pl.program_id(2)
    j = pl.program_id(3)
    nk = pl.num_programs(3)
    half = N_BIAS_TILES // 2

    @pl.when(j == 0)
    def _():
        m_scr[...] = jnp.full(m_scr.shape, -jnp.inf, F32)
        l_scr[...] = jnp.zeros(l_scr.shape, F32)
        acc_scr[...] = jnp.zeros(acc_scr.shape, F32)

    bias = bias_ref[0, jnp.clip(j - i, -half, half) + half]
    q = q_ref[0]
    k = k_ref[0]
    v = v_ref[0]
    for t in range(2):
        hs = slice(t * HEAD_DIM, (t + 1) * HEAD_DIM)
        s = lax.dot_general(q[:, hs], k[:, hs], (((1,), (1,)), ((), ())),
                            preferred_element_type=F32) + bias
        m_prev = m_scr[t]
        m_new = jnp.maximum(m_prev, jnp.max(s, axis=-1, keepdims=True))
        alpha = jnp.exp(m_prev - m_new)
        p = jnp.exp(s - m_new)
        l_scr[t] = alpha * l_scr[t] + jnp.sum(p, axis=-1, keepdims=True)
        acc_scr[t] = alpha * acc_scr[t] + _dot(p.astype(BF16), v)
        m_scr[t] = m_new

    @pl.when(j == nk - 1)
    def _():
        lv = lam_ref[...]
        lam = (jnp.exp(jnp.sum(lv[0:1, :] * lv[1:2, :], axis=-1, keepdims=True))
               - jnp.exp(jnp.sum(lv[2:3, :] * lv[3:4, :], axis=-1, keepdims=True))
               + lambda_init)
        o = acc_scr[0] / l_scr[0] - lam * (acc_scr[1] / l_scr[1])
        o = _rms(o, sg_ref[...]) * (1.0 - lambda_init)
        o_ref[0] = o.astype(BF16)


def _attention(q, k, v, bias_tiles, lam_vecs, subln_g, lambda_init):
    b, s, d = q.shape
    t = ATTN_TILE
    nt = s // t
    hw = 2 * HEAD_DIM
    qo_spec = pl.BlockSpec((1, t, hw), lambda h, bi, i, j: (bi, i, h))
    kv_spec = pl.BlockSpec((1, t, hw), lambda h, bi, i, j: (bi, j, h))
    return pl.pallas_call(
        functools.partial(_attn_kernel, lambda_init=lambda_init),
        out_shape=jax.ShapeDtypeStruct((b, s, d), BF16),
        grid=(N_HEADS, b, nt, nt),
        in_specs=[
            qo_spec, kv_spec, kv_spec,
            pl.BlockSpec((1, N_BIAS_TILES, t, t), lambda h, bi, i, j: (h, 0, 0, 0)),
            pl.BlockSpec((4, HEAD_DIM), lambda h, bi, i, j: (0, 0)),
            pl.BlockSpec((1, hw), lambda h, bi, i, j: (0, 0)),
        ],
        out_specs=qo_spec,
        scratch_shapes=[
            pltpu.VMEM((2, t, 1), F32),
            pltpu.VMEM((2, t, 1), F32),
            pltpu.VMEM((2, t, hw), F32),
        ],
        compiler_params=pltpu.CompilerParams(
            dimension_semantics=("parallel", "parallel", "parallel", "arbitrary"),
            vmem_limit_bytes=_vmem_limit(2 * N_BIAS_TILES * t * t * 4 + (24 << 20))),
        name="diff_attention",
    )(q, k, v, bias_tiles, lam_vecs, subln_g)


def _oproj_kernel(x_ref, a_ref, g_ref, w_ref, o_ref):
    y = _dot(a_ref[...], w_ref[...])
    o_ref[...] = x_ref[...] + _rms(y, g_ref[1:2, :])


def _out_proj(x2, a2, g, w_o):
    n_tok, d = x2.shape
    tm = TOKEN_TILE
    tile = pl.BlockSpec((tm, d), lambda i: (i, 0))
    return pl.pallas_call(
        _oproj_kernel,
        out_shape=jax.ShapeDtypeStruct(x2.shape, F32),
        grid=(n_tok // tm,),
        in_specs=[tile, tile, _const_spec((4, d)), _const_spec((d, d))],
        out_specs=tile,
        compiler_params=pltpu.CompilerParams(
            dimension_semantics=("parallel",),
            vmem_limit_bytes=_vmem_limit(d * d * 2 + (24 << 20))),
        name="attn_out_proj",
    )(x2, a2, g, w_o)


def kernel(x, norm_g, conv_w_in, conv_w, conv_w_out, pool_w, pool_scale, attn_w_qkv, attn_w_o,
           lambda_q1, lambda_k1, lambda_q2, lambda_k2, attn_subln_g, rel_bias,
           ffn_w_gate, ffn_w_up, ffn_w_down):
    b, s, d = x.shape
    bias_tiles = None
    ia = ib = ic = 0
    for i in range(DEPTH):
        g = norm_g[i]
        kind = i % N_MIXERS
        if kind == 0:
            x = _conv_mixer(x, g, conv_w_in[ia].astype(BF16), conv_w[ia],
                            conv_w_out[ia].astype(BF16))
            ia += 1
        elif kind == 1:
            x = _pool_mixer(x, g, pool_w[ib].astype(BF16), pool_scale[ib].reshape(1, d))
            ib += 1
        else:
            if bias_tiles is None:
                bias_tiles = _bias_tiles(rel_bias, ATTN_TILE)
            x2 = x.reshape(b * s, d)
            q, k, v = _qkv_proj(x2, g, attn_w_qkv[ic].astype(BF16))
            lam_vecs = jnp.stack([lambda_q1[ic], lambda_k1[ic], lambda_q2[ic], lambda_k2[ic]])
            a = _attention(q.reshape(b, s, d), k.reshape(b, s, d), v.reshape(b, s, d),
                           bias_tiles, lam_vecs, attn_subln_g[ic].reshape(1, 2 * HEAD_DIM),
                           _lambda_init(i))
            x = _out_proj(x2, a.reshape(b * s, d), g, attn_w_o[ic].astype(BF16)).reshape(b, s, d)
            ic += 1
        x = _ffn(x.reshape(b * s, d), g, ffn_w_gate[i].astype(BF16), ffn_w_up[i].astype(BF16),
                 ffn_w_down[i].astype(BF16)).reshape(b, s, d)
    return x
```

```python
import functools
import math

import jax
import jax.numpy as jnp
from jax import lax
from jax.experimental import pallas as pl
from jax.experimental.pallas import tpu as pltpu

D_MODEL = 1024
DEPTH = 4
N_MIXERS = 3
CONV_WIDTH = 3
POOL_WINDOWS = (2, 4, 8, 16)
POOL_GROUP_CH = D_MODEL // len(POOL_WINDOWS)
HEAD_DIM = 128
N_HEADS = D_MODEL // (2 * HEAD_DIM)
NUM_BUCKETS = 32
MAX_DISTANCE = 128
D_FF = ((8 * D_MODEL // 3 + 255) // 256) * 256
EPS = 1e-6

F32 = jnp.float32
BF16 = jnp.bfloat16

SUBLANES = 8
LANES = 128
MXU_DIM = 256
VMEM_BYTES = 64 * 1024 * 1024

TOKEN_TILE = 512
POOL_TILE = 256
ATTN_TILE = 512
FF_CHUNK = MXU_DIM
CONV_CHUNK = MXU_DIM
HALO_PAD = LANES


def _lambda_init(layer_idx):
    return 0.8 - 0.6 * math.exp(-0.3 * layer_idx)


def _vmem_limit(nbytes):
    return int(min(VMEM_BYTES - (4 << 20), nbytes))


def _rms(x, g):
    ms = jnp.mean(x * x, axis=-1, keepdims=True)
    return (x * lax.rsqrt(ms + EPS)) * g


def _dot(a, b):
    return jnp.dot(a, b, preferred_element_type=F32)


def _const_spec(shape):
    return pl.BlockSpec(shape, lambda *_: (0,) * len(shape), pipeline_mode=pl.Buffered(1))


def _ffn_kernel(x_ref, g_ref, wg_ref, wu_ref, wd_ref, o_ref, h_scr):
    x = x_ref[...]
    xn = _rms(x, g_ref[2:3, :]).astype(BF16)
    for c in range(0, D_FF, FF_CHUNK):
        gate = _dot(xn, wg_ref[:, c:c + FF_CHUNK])
        up = _dot(xn, wu_ref[:, c:c + FF_CHUNK])
        h_scr[:, c:c + FF_CHUNK] = (jax.nn.silu(gate) * up).astype(BF16)
    f = _dot(h_scr[...], wd_ref[...])
    o_ref[...] = x + _rms(f, g_ref[3:4, :])


def _ffn(x2, g, wg, wu, wd):
    n_tok = x2.shape[0]
    tm = TOKEN_TILE
    weights = 3 * D_MODEL * D_FF * 2
    tiles = 4 * tm * D_MODEL * 4 + tm * D_FF * 2
    return pl.pallas_call(
        _ffn_kernel,
        out_shape=jax.ShapeDtypeStruct(x2.shape, F32),
        grid=(n_tok // tm,),
        in_specs=[
            pl.BlockSpec((tm, D_MODEL), lambda i: (i, 0)),
            _const_spec((4, D_MODEL)),
            _const_spec((D_MODEL, D_FF)),
            _const_spec((D_MODEL, D_FF)),
            _const_spec((D_FF, D_MODEL)),
        ],
        out_specs=pl.BlockSpec((tm, D_MODEL), lambda i: (i, 0)),
        scratch_shapes=[pltpu.VMEM((tm, D_FF), BF16)],
        compiler_params=pltpu.CompilerParams(
            dimension_semantics=("parallel",),
            vmem_limit_bytes=_vmem_limit(weights + tiles + (16 << 20))),
        name="ffn",
    )(x2, g, wg, wu, wd)


def _conv_kernel(xp_ref, x_ref, xn_ref, g_ref, win_ref, cw_ref, wout_ref, o_ref, v_scr):
    i = pl.program_id(1)
    last = pl.num_programs(1) - 1
    tm = x_ref.shape[1]
    d = D_MODEL
    x = x_ref[0]
    g0 = g_ref[0:1, :]
    hn = _rms(x, g0).astype(BF16)
    halo = jnp.concatenate([xp_ref[0], xn_ref[0]], axis=0)
    hh = _rms(halo, g0).astype(BF16)
    row = lax.broadcasted_iota(jnp.int32, (tm, 1), 0)
    for c in range(0, d, CONV_CHUNK):
        cs = slice(c, c + CONV_CHUNK)
        gate_b = _dot(hn, win_ref[:, c:c + CONV_CHUNK])
        u = (_dot(hn, win_ref[:, d + c:d + c + CONV_CHUNK])
             * _dot(hn, win_ref[:, 2 * d + c:2 * d + c + CONV_CHUNK]))
        uh = (_dot(hh, win_ref[:, d + c:d + c + CONV_CHUNK])
              * _dot(hh, win_ref[:, 2 * d + c:2 * d + c + CONV_CHUNK]))
        u_prev = jnp.where(i > 0, uh[SUBLANES - 1:SUBLANES, :], 0.0)
        u_next = jnp.where(i < last, uh[SUBLANES:SUBLANES + 1, :], 0.0)
        u_dn = jnp.where(row == 0, u_prev, pltpu.roll(u, 1, axis=0))
        u_up = jnp.where(row == tm - 1, u_next, pltpu.roll(u, tm - 1, axis=0))
        conv = cw_ref[0:1, cs] * u_dn + cw_ref[1:2, cs] * u + cw_ref[2:3, cs] * u_up
        v_scr[:, cs] = (gate_b * conv).astype(BF16)
    y = _dot(v_scr[...], wout_ref[...])
    o_ref[0] = x + _rms(y, g_ref[1:2, :])


def _conv_mixer(x, g, w_in, w_conv, w_out):
    b, s, d = x.shape
    tm = TOKEN_TILE
    nt = s // tm
    hb = tm // SUBLANES
    weights = (3 * d * d + d * d) * 2
    tiles = 4 * tm * d * 4 + tm * d * 2
    return pl.pallas_call(
        _conv_kernel,
        out_shape=jax.ShapeDtypeStruct(x.shape, F32),
        grid=(b, nt),
        in_specs=[
            pl.BlockSpec((1, SUBLANES, d), lambda bi, i: (bi, jnp.maximum(i * hb - 1, 0), 0)),
            pl.BlockSpec((1, tm, d), lambda bi, i: (bi, i, 0)),
            pl.BlockSpec((1, SUBLANES, d),
                         lambda bi, i: (bi, jnp.minimum((i + 1) * hb, s // SUBLANES - 1), 0)),
            _const_spec((4, d)),
            _const_spec((d, 3 * d)),
            _const_spec((CONV_WIDTH, d)),
            _const_spec((d, d)),
        ],
        out_specs=pl.BlockSpec((1, tm, d), lambda bi, i: (bi, i, 0)),
        scratch_shapes=[pltpu.VMEM((tm, d), BF16)],
        compiler_params=pltpu.CompilerParams(
            dimension_semantics=("parallel", "parallel"),
            vmem_limit_bytes=_vmem_limit(weights + tiles + (16 << 20))),
        name="conv_mixer",
    )(x, x, x, g, w_in, w_conv, w_out)


def _pool_kernel(xp_ref, x_ref, xn_ref, g_ref, wp_ref, ps_ref, o_ref, *, seq_len):
    i = pl.program_id(1)
    last = pl.num_programs(1) - 1
    tm = x_ref.shape[1]
    te = tm + HALO_PAD
    x = x_ref[0]
    g0 = g_ref[0:1, :]
    ext = jnp.concatenate(
        [xp_ref[0], x, xn_ref[0], jnp.zeros((HALO_PAD - 2 * SUBLANES, D_MODEL), F32)], axis=0)
    erow = lax.broadcasted_iota(jnp.int32, (te, 1), 0)
    inside = ((erow >= SUBLANES) | (i > 0)) & ((erow < tm + SUBLANES) | (i < last))
    hn_ext = jnp.where(inside, _rms(ext, g0), 0.0)
    hn_ext_b = hn_ext.astype(BF16)
    hn = hn_ext[SUBLANES:SUBLANES + tm, :]
    rel = (lax.broadcasted_iota(jnp.int32, (tm, te), 1) - SUBLANES
           - lax.broadcasted_iota(jnp.int32, (tm, te), 0))
    t = i * tm + lax.broadcasted_iota(jnp.int32, (tm, 1), 0)
    outs = []
    for gi, w in enumerate(POOL_WINDOWS):
        lo_off, hi_off = -(w // 2), w - 1 - w // 2
        cs = slice(gi * POOL_GROUP_CH, (gi + 1) * POOL_GROUP_CH)
        band = jnp.where((rel >= lo_off) & (rel <= hi_off), 1.0, 0.0).astype(BF16)
        win_sum = _dot(band, hn_ext_b[:, cs])
        cnt = (jnp.minimum(t + hi_off, seq_len - 1) - jnp.maximum(t + lo_off, 0) + 1).astype(F32)
        p = win_sum / cnt - hn[:, cs]
        outs.append(_dot(p.astype(BF16), wp_ref[gi]))
    y = jnp.concatenate(outs, axis=1) * ps_ref[...]
    o_ref[0] = x + _rms(y, g_ref[1:2, :])


def _pool_mixer(x, g, w_pool, scale):
    b, s, d = x.shape
    tm = POOL_TILE
    nt = s // tm
    hb = tm // SUBLANES
    ng = len(POOL_WINDOWS)
    return pl.pallas_call(
        functools.partial(_pool_kernel, seq_len=s),
        out_shape=jax.ShapeDtypeStruct(x.shape, F32),
        grid=(b, nt),
        in_specs=[
            pl.BlockSpec((1, SUBLANES, d), lambda bi, i: (bi, jnp.maximum(i * hb - 1, 0), 0)),
            pl.BlockSpec((1, tm, d), lambda bi, i: (bi, i, 0)),
            pl.BlockSpec((1, SUBLANES, d),
                         lambda bi, i: (bi, jnp.minimum((i + 1) * hb, s // SUBLANES - 1), 0)),
            _const_spec((4, d)),
            _const_spec((ng, POOL_GROUP_CH, POOL_GROUP_CH)),
            _const_spec((1, d)),
        ],
        out_specs=pl.BlockSpec((1, tm, d), lambda bi, i: (bi, i, 0)),
        compiler_params=pltpu.CompilerParams(
            dimension_semantics=("parallel", "parallel"),
            vmem_limit_bytes=_vmem_limit(32 << 20)),
        name="pool_mixer",
    )(x, x, x, g, w_pool, scale)


_NT_DIMS = (((1,), (1,)), ((), ()))
LOG2E = math.log2(math.e)


def _qkv_kernel(x_ref, g_ref, wqk_ref, wvt_ref, q_ref, k_ref, vt_ref):
    d = D_MODEL
    hn = _rms(x_ref[0], g_ref[0:1, :]).astype(BF16)
    q_ref[0] = (_dot(hn, wqk_ref[:, :d]) * (HEAD_DIM ** -0.5 * LOG2E)).astype(BF16)
    k_ref[0] = _dot(hn, wqk_ref[:, d:]).astype(BF16)
    vt_ref[0] = lax.dot_general(wvt_ref[...], hn, _NT_DIMS,
                                preferred_element_type=F32).astype(BF16)


def _qkv_proj(x, g, w_qk, w_vt):
    b, s, d = x.shape
    tm = TOKEN_TILE
    tile = pl.BlockSpec((1, tm, d), lambda bi, i: (bi, i, 0))
    return pl.pallas_call(
        _qkv_kernel,
        out_shape=(jax.ShapeDtypeStruct((b, s, d), BF16), jax.ShapeDtypeStruct((b, s, d), BF16),
                   jax.ShapeDtypeStruct((b, d, s), BF16)),
        grid=(b, s // tm),
        in_specs=[tile, _const_spec((4, d)), _const_spec((d, 2 * d)), _const_spec((d, d))],
        out_specs=(tile, tile, pl.BlockSpec((1, d, tm), lambda bi, i: (bi, 0, i))),
        compiler_params=pltpu.CompilerParams(
            dimension_semantics=("parallel", "parallel"),
            vmem_limit_bytes=_vmem_limit(3 * d * d * 2 + (24 << 20))),
        name="qkv_proj",
    )(x, g, w_qk, w_vt)


def _t5_bucket(rel):
    half = NUM_BUCKETS // 2
    ret = jnp.where(rel > 0, half, 0)
    n = jnp.abs(rel)
    max_exact = half // 2
    nf = jnp.maximum(n, 1).astype(F32)
    large = max_exact + (jnp.log(nf / max_exact) / math.log(MAX_DISTANCE / max_exact)
                         * (half - max_exact)).astype(jnp.int32)
    large = jnp.minimum(large, half - 1)
    return ret + jnp.where(n < max_exact, n, large)


N_BIAS_TILES = 5


def _bias_tile_kernel(rb_ref, bkt_ref, o_ref):
    h = pl.program_id(0)
    t = o_ref.shape[-1]
    bkt = bkt_ref[0]
    val = jnp.zeros(bkt.shape, F32)
    for b in range(NUM_BUCKETS):
        val = jnp.where(bkt == b, rb_ref[b, h], val)
    full = jnp.broadcast_to(val * LOG2E, (t, 2 * t))
    skew = pltpu.roll(full, 0, 1, stride=1, stride_axis=0)
    o_ref[0, 0] = skew[:, :t]


def _bias_tiles(rel_bias, t):
    m = jnp.arange(2 * t, dtype=jnp.int32)
    q_minus_k = jnp.where(m < t, m, m - 2 * t)
    half = N_BIAS_TILES // 2
    tile_off = jnp.arange(-half, half + 1, dtype=jnp.int32)
    bkt = _t5_bucket(tile_off[:, None] * t - q_minus_k[None, :]).reshape(N_BIAS_TILES, 1, 2 * t)
    return pl.pallas_call(
        _bias_tile_kernel,
        out_shape=jax.ShapeDtypeStruct((N_HEADS, N_BIAS_TILES, t, t), F32),
        grid=(N_HEADS, N_BIAS_TILES),
        in_specs=[
            pl.BlockSpec(memory_space=pltpu.SMEM),
            pl.BlockSpec((1, 1, 2 * t), lambda h, k: (k, 0, 0)),
        ],
        out_specs=pl.BlockSpec((1, 1, t, t), lambda h, k: (h, k, 0, 0)),
        compiler_params=pltpu.CompilerParams(dimension_semantics=("parallel", "parallel")),
        name="rel_bias_tiles",
    )(rel_bias, bkt)


def _attn_kernel(q_ref, k_ref, vt_ref, bias_ref, lam_ref, sg_ref, o_ref, *, lambda_init):
    i = pl.program_id(2)
    half = N_BIAS_TILES // 2
    q = q_ref[0]
    t_blk = q.shape[0]
    nk = k_ref.shape[1] // t_blk
    m = [None, None]
    l = [None, None]
    acc = [None, None]
    for j in range(nk):
        ks = slice(j * t_blk, (j + 1) * t_blk)
        bias = bias_ref[0, jnp.clip(j - i, -half, half) + half]
        k = k_ref[0, ks, :]
        vt = vt_ref[0, :, ks]
        for t in range(2):
            hs = slice(t * HEAD_DIM, (t + 1) * HEAD_DIM)
            s = lax.dot_general(k[:, hs], q[:, hs], _NT_DIMS, preferred_element_type=F32) + bias
            m_cur = jnp.max(s, axis=0, keepdims=True)
            if j == 0:
                m[t] = m_cur
                p = jnp.exp2(s - m_cur)
                l[t] = jnp.sum(p, axis=0, keepdims=True)
                acc[t] = _dot(vt, p.astype(BF16))
            else:
                m_new = jnp.maximum(m[t], m_cur)
                alpha = jnp.exp2(m[t] - m_new)
                p = jnp.exp2(s - m_new)
                l[t] = alpha * l[t] + jnp.sum(p, axis=0, keepdims=True)
                acc[t] = alpha * acc[t] + _dot(vt, p.astype(BF16))
                m[t] = m_new

    lv = lam_ref[...]
    lam = (jnp.exp(jnp.sum(lv[0:1, :] * lv[1:2, :], axis=-1, keepdims=True))
           - jnp.exp(jnp.sum(lv[2:3, :] * lv[3:4, :], axis=-1, keepdims=True))
           + lambda_init)
    o = acc[0] / l[0] - lam * (acc[1] / l[1])
    ms = jnp.mean(o * o, axis=0, keepdims=True)
    o = (o * lax.rsqrt(ms + EPS)) * sg_ref[...] * (1.0 - lambda_init)
    o_ref[0] = o.T.astype(BF16)


def _attention(q, k, vt, bias_tiles, lam_vecs, subln_g, lambda_init):
    b, s, d = q.shape
    t = ATTN_TILE
    hw = 2 * HEAD_DIM
    qo_spec = pl.BlockSpec((1, t, hw), lambda h, bi, i: (bi, i, h))
    return pl.pallas_call(
        functools.partial(_attn_kernel, lambda_init=lambda_init),
        out_shape=jax.ShapeDtypeStruct((b, s, d), BF16),
        grid=(N_HEADS, b, s // t),
        in_specs=[
            qo_spec,
            pl.BlockSpec((1, s, hw), lambda h, bi, i: (bi, 0, h)),
            pl.BlockSpec((1, hw, s), lambda h, bi, i: (bi, h, 0)),
            pl.BlockSpec((1, N_BIAS_TILES, t, t), lambda h, bi, i: (h, 0, 0, 0)),
            pl.BlockSpec((4, HEAD_DIM), lambda h, bi, i: (0, 0)),
            pl.BlockSpec((hw, 1), lambda h, bi, i: (0, 0)),
        ],
        out_specs=qo_spec,
        compiler_params=pltpu.CompilerParams(
            dimension_semantics=("parallel", "parallel", "parallel"),
            vmem_limit_bytes=_vmem_limit(2 * N_BIAS_TILES * t * t * 4 + 8 * s * hw + (24 << 20))),
        name="diff_attention",
    )(q, k, vt, bias_tiles, lam_vecs, subln_g)


def _oproj_kernel(x_ref, a_ref, g_ref, w_ref, o_ref):
    y = _dot(a_ref[...], w_ref[...])
    o_ref[...] = x_ref[...] + _rms(y, g_ref[1:2, :])


def _out_proj(x2, a2, g, w_o):
    n_tok, d = x2.shape
    tm = TOKEN_TILE
    tile = pl.BlockSpec((tm, d), lambda i: (i, 0))
    return pl.pallas_call(
        _oproj_kernel,
        out_shape=jax.ShapeDtypeStruct(x2.shape, F32),
        grid=(n_tok // tm,),
        in_specs=[tile, tile, _const_spec((4, d)), _const_spec((d, d))],
        out_specs=tile,
        compiler_params=pltpu.CompilerParams(
            dimension_semantics=("parallel",),
            vmem_limit_bytes=_vmem_limit(d * d * 2 + (24 << 20))),
        name="attn_out_proj",
    )(x2, a2, g, w_o)


def kernel(x, norm_g, conv_w_in, conv_w, conv_w_out, pool_w, pool_scale, attn_w_qkv, attn_w_o,
           lambda_q1, lambda_k1, lambda_q2, lambda_k2, attn_subln_g, rel_bias,
           ffn_w_gate, ffn_w_up, ffn_w_down):
    b, s, d = x.shape
    bias_tiles = None
    ia = ib = ic = 0
    for i in range(DEPTH):
        g = norm_g[i]
        kind = i % N_MIXERS
        if kind == 0:
            x = _conv_mixer(x, g, conv_w_in[ia].astype(BF16), conv_w[ia],
                            conv_w_out[ia].astype(BF16))
            ia += 1
        elif kind == 1:
            x = _pool_mixer(x, g, pool_w[ib].astype(BF16), pool_scale[ib].reshape(1, d))
            ib += 1
        else:
            if bias_tiles is None:
                bias_tiles = _bias_tiles(rel_bias, ATTN_TILE)
            w_qkv = attn_w_qkv[ic]
            q, k, vt = _qkv_proj(x, g, w_qkv[:, :2 * d].astype(BF16),
                                 w_qkv[:, 2 * d:].T.astype(BF16))
            lam_vecs = jnp.stack([lambda_q1[ic], lambda_k1[ic], lambda_q2[ic], lambda_k2[ic]])
            a = _attention(q, k, vt, bias_tiles, lam_vecs,
                           attn_subln_g[ic].reshape(2 * HEAD_DIM, 1), _lambda_init(i))
            x = _out_proj(x.reshape(b * s, d), a.reshape(b * s, d), g,
                          attn_w_o[ic].astype(BF16)).reshape(b, s, d)
            ic += 1
        x = _ffn(x.reshape(b * s, d), g, ffn_w_gate[i].astype(BF16), ffn_w_up[i].astype(BF16),
                 ffn_w_down[i].astype(BF16)).reshape(b, s, d)
    return x
```

```python
import functools
import math

import jax
import jax.numpy as jnp
from jax import lax
from jax.experimental import pallas as pl
from jax.experimental.pallas import tpu as pltpu

D_MODEL = 1024
DEPTH = 4
N_MIXERS = 3
CONV_WIDTH = 3
POOL_WINDOWS = (2, 4, 8, 16)
POOL_GROUP_CH = D_MODEL // len(POOL_WINDOWS)
HEAD_DIM = 128
N_HEADS = D_MODEL // (2 * HEAD_DIM)
NUM_BUCKETS = 32
MAX_DISTANCE = 128
D_FF = ((8 * D_MODEL // 3 + 255) // 256) * 256
EPS = 1e-6

F32 = jnp.float32
BF16 = jnp.bfloat16

SUBLANES = 8
LANES = 128
MXU_DIM = 256
VMEM_BYTES = 64 * 1024 * 1024

TOKEN_TILE = 512
POOL_TILE = 256
ATTN_TILE = 512
FF_CHUNK = MXU_DIM
CONV_CHUNK = MXU_DIM
HALO_PAD = LANES


def _lambda_init(layer_idx):
    return 0.8 - 0.6 * math.exp(-0.3 * layer_idx)


def _vmem_limit(nbytes):
    return int(min(VMEM_BYTES - (4 << 20), nbytes))


def _rms(x, g):
    ms = jnp.mean(x * x, axis=-1, keepdims=True)
    return (x * lax.rsqrt(ms + EPS)) * g


def _dot(a, b):
    return jnp.dot(a, b, preferred_element_type=F32)


def _const_spec(shape):
    return pl.BlockSpec(shape, lambda *_: (0,) * len(shape), pipeline_mode=pl.Buffered(1))


def _ffn_kernel(x_ref, g_ref, wg_ref, wu_ref, wd_ref, o_ref, h_scr):
    x = x_ref[...]
    xn = _rms(x, g_ref[2:3, :]).astype(BF16)
    for c in range(0, D_FF, FF_CHUNK):
        gate = _dot(xn, wg_ref[:, c:c + FF_CHUNK])
        up = _dot(xn, wu_ref[:, c:c + FF_CHUNK])
        h_scr[:, c:c + FF_CHUNK] = (jax.nn.silu(gate) * up).astype(BF16)
    f = _dot(h_scr[...], wd_ref[...])
    o_ref[...] = x + _rms(f, g_ref[3:4, :])


def _ffn(x2, g, wg, wu, wd):
    n_tok = x2.shape[0]
    tm = TOKEN_TILE
    weights = 3 * D_MODEL * D_FF * 2
    tiles = 4 * tm * D_MODEL * 4 + tm * D_FF * 2
    return pl.pallas_call(
        _ffn_kernel,
        out_shape=jax.ShapeDtypeStruct(x2.shape, F32),
        grid=(n_tok // tm,),
        in_specs=[
            pl.BlockSpec((tm, D_MODEL), lambda i: (i, 0)),
            _const_spec((4, D_MODEL)),
            _const_spec((D_MODEL, D_FF)),
            _const_spec((D_MODEL, D_FF)),
            _const_spec((D_FF, D_MODEL)),
        ],
        out_specs=pl.BlockSpec((tm, D_MODEL), lambda i: (i, 0)),
        scratch_shapes=[pltpu.VMEM((tm, D_FF), BF16)],
        compiler_params=pltpu.CompilerParams(
            dimension_semantics=("parallel",),
            vmem_limit_bytes=_vmem_limit(weights + tiles + (16 << 20))),
        name="ffn",
    )(x2, g, wg, wu, wd)


def _conv_kernel(xp_ref, x_ref, xn_ref, g_ref, win_ref, cw_ref, wout_ref, o_ref, v_scr):
    i = pl.program_id(1)
    last = pl.num_programs(1) - 1
    tm = x_ref.shape[1]
    d = D_MODEL
    x = x_ref[0]
    g0 = g_ref[0:1, :]
    hn = _rms(x, g0).astype(BF16)
    halo = jnp.concatenate([xp_ref[0], xn_ref[0]], axis=0)
    hh = _rms(halo, g0).astype(BF16)
    row = lax.broadcasted_iota(jnp.int32, (tm, 1), 0)
    for c in range(0, d, CONV_CHUNK):
        cs = slice(c, c + CONV_CHUNK)
        gate_b = _dot(hn, win_ref[:, c:c + CONV_CHUNK])
        u = (_dot(hn, win_ref[:, d + c:d + c + CONV_CHUNK])
             * _dot(hn, win_ref[:, 2 * d + c:2 * d + c + CONV_CHUNK]))
        uh = (_dot(hh, win_ref[:, d + c:d + c + CONV_CHUNK])
              * _dot(hh, win_ref[:, 2 * d + c:2 * d + c + CONV_CHUNK]))
        u_prev = jnp.where(i > 0, uh[SUBLANES - 1:SUBLANES, :], 0.0)
        u_next = jnp.where(i < last, uh[SUBLANES:SUBLANES + 1, :], 0.0)
        u_dn = jnp.where(row == 0, u_prev, pltpu.roll(u, 1, axis=0))
        u_up = jnp.where(row == tm - 1, u_next, pltpu.roll(u, tm - 1, axis=0))
        conv = cw_ref[0:1, cs] * u_dn + cw_ref[1:2, cs] * u + cw_ref[2:3, cs] * u_up
        v_scr[:, cs] = (gate_b * conv).astype(BF16)
    y = _dot(v_scr[...], wout_ref[...])
    o_ref[0] = x + _rms(y, g_ref[1:2, :])


def _conv_mixer(x, g, w_in, w_conv, w_out):
    b, s, d = x.shape
    tm = TOKEN_TILE
    nt = s // tm
    hb = tm // SUBLANES
    weights = (3 * d * d + d * d) * 2
    tiles = 4 * tm * d * 4 + tm * d * 2
    return pl.pallas_call(
        _conv_kernel,
        out_shape=jax.ShapeDtypeStruct(x.shape, F32),
        grid=(b, nt),
        in_specs=[
            pl.BlockSpec((1, SUBLANES, d), lambda bi, i: (bi, jnp.maximum(i * hb - 1, 0), 0)),
            pl.BlockSpec((1, tm, d), lambda bi, i: (bi, i, 0)),
            pl.BlockSpec((1, SUBLANES, d),
                         lambda bi, i: (bi, jnp.minimum((i + 1) * hb, s // SUBLANES - 1), 0)),
            _const_spec((4, d)),
            _const_spec((d, 3 * d)),
            _const_spec((CONV_WIDTH, d)),
            _const_spec((d, d)),
        ],
        out_specs=pl.BlockSpec((1, tm, d), lambda bi, i: (bi, i, 0)),
        scratch_shapes=[pltpu.VMEM((tm, d), BF16)],
        compiler_params=pltpu.CompilerParams(
            dimension_semantics=("parallel", "parallel"),
            vmem_limit_bytes=_vmem_limit(weights + tiles + (16 << 20))),
        name="conv_mixer",
    )(x, x, x, g, w_in, w_conv, w_out)


def _pool_kernel(xp_ref, x_ref, xn_ref, g_ref, wp_ref, ps_ref, o_ref, *, seq_len):
    i = pl.program_id(1)
    last = pl.num_programs(1) - 1
    tm = x_ref.shape[1]
    te = tm + HALO_PAD
    x = x_ref[0]
    g0 = g_ref[0:1, :]
    ext = jnp.concatenate(
        [xp_ref[0], x, xn_ref[0], jnp.zeros((HALO_PAD - 2 * SUBLANES, D_MODEL), F32)], axis=0)
    erow = lax.broadcasted_iota(jnp.int32, (te, 1), 0)
    inside = ((erow >= SUBLANES) | (i > 0)) & ((erow < tm + SUBLANES) | (i < last))
    hn_ext = jnp.where(inside, _rms(ext, g0), 0.0)
    hn_ext_b = hn_ext.astype(BF16)
    hn = hn_ext[SUBLANES:SUBLANES + tm, :]
    rel = (lax.broadcasted_iota(jnp.int32, (tm, te), 1) - SUBLANES
           - lax.broadcasted_iota(jnp.int32, (tm, te), 0))
    t = i * tm + lax.broadcasted_iota(jnp.int32, (tm, 1), 0)
    outs = []
    for gi, w in enumerate(POOL_WINDOWS):
        lo_off, hi_off = -(w // 2), w - 1 - w // 2
        cs = slice(gi * POOL_GROUP_CH, (gi + 1) * POOL_GROUP_CH)
        band = jnp.where((rel >= lo_off) & (rel <= hi_off), 1.0, 0.0).astype(BF16)
        win_sum = _dot(band, hn_ext_b[:, cs])
        cnt = (jnp.minimum(t + hi_off, seq_len - 1) - jnp.maximum(t + lo_off, 0) + 1).astype(F32)
        p = win_sum / cnt - hn[:, cs]
        outs.append(_dot(p.astype(BF16), wp_ref[gi]))
    y = jnp.concatenate(outs, axis=1) * ps_ref[...]
    o_ref[0] = x + _rms(y, g_ref[1:2, :])


def _pool_mixer(x, g, w_pool, scale):
    b, s, d = x.shape
    tm = POOL_TILE
    nt = s // tm
    hb = tm // SUBLANES
    ng = len(POOL_WINDOWS)
    return pl.pallas_call(
        functools.partial(_pool_kernel, seq_len=s),
        out_shape=jax.ShapeDtypeStruct(x.shape, F32),
        grid=(b, nt),
        in_specs=[
            pl.BlockSpec((1, SUBLANES, d), lambda bi, i: (bi, jnp.maximum(i * hb - 1, 0), 0)),
            pl.BlockSpec((1, tm, d), lambda bi, i: (bi, i, 0)),
            pl.BlockSpec((1, SUBLANES, d),
                         lambda bi, i: (bi, jnp.minimum((i + 1) * hb, s // SUBLANES - 1), 0)),
            _const_spec((4, d)),
            _const_spec((ng, POOL_GROUP_CH, POOL_GROUP_CH)),
            _const_spec((1, d)),
        ],
        out_specs=pl.BlockSpec((1, tm, d), lambda bi, i: (bi, i, 0)),
        compiler_params=pltpu.CompilerParams(
            dimension_semantics=("parallel", "parallel"),
            vmem_limit_bytes=_vmem_limit(32 << 20)),
        name="pool_mixer",
    )(x, x, x, g, w_pool, scale)


_NT_DIMS = (((1,), (1,)), ((), ()))
LOG2E = math.log2(math.e)


def _qkv_kernel(x_ref, g_ref, wqk_ref, wvt_ref, q_ref, k_ref, vt_ref):
    d = D_MODEL
    hn = _rms(x_ref[0], g_ref[0:1, :]).astype(BF16)
    q_ref[0] = (_dot(hn, wqk_ref[:, :d]) * (HEAD_DIM ** -0.5 * LOG2E)).astype(BF16)
    k_ref[0] = _dot(hn, wqk_ref[:, d:]).astype(BF16)
    vt_ref[0] = lax.dot_general(wvt_ref[...], hn, _NT_DIMS,
                                preferred_element_type=F32).astype(BF16)


def _qkv_proj(x, g, w_qk, w_vt):
    b, s, d = x.shape
    tm = TOKEN_TILE
    tile = pl.BlockSpec((1, tm, d), lambda bi, i: (bi, i, 0))
    return pl.pallas_call(
        _qkv_kernel,
        out_shape=(jax.ShapeDtypeStruct((b, s, d), BF16), jax.ShapeDtypeStruct((b, s, d), BF16),
                   jax.ShapeDtypeStruct((b, d, s), BF16)),
        grid=(b, s // tm),
        in_specs=[tile, _const_spec((4, d)), _const_spec((d, 2 * d)), _const_spec((d, d))],
        out_specs=(tile, tile, pl.BlockSpec((1, d, tm), lambda bi, i: (bi, 0, i))),
        compiler_params=pltpu.CompilerParams(
            dimension_semantics=("parallel", "parallel"),
            vmem_limit_bytes=_vmem_limit(3 * d * d * 2 + (24 << 20))),
        name="qkv_proj",
    )(x, g, w_qk, w_vt)


def _t5_bucket(rel):
    half = NUM_BUCKETS // 2
    ret = jnp.where(rel > 0, half, 0)
    n = jnp.abs(rel)
    max_exact = half // 2
    nf = jnp.maximum(n, 1).astype(F32)
    large = max_exact + (jnp.log(nf / max_exact) / math.log(MAX_DISTANCE / max_exact)
                         * (half - max_exact)).astype(jnp.int32)
    large = jnp.minimum(large, half - 1)
    return ret + jnp.where(n < max_exact, n, large)


N_BIAS_TILES = 5


def _bias_tile_kernel(rb_ref, bkt_ref, o_ref):
    h = pl.program_id(0)
    t = o_ref.shape[-1]
    bkt = bkt_ref[0]
    val = jnp.zeros(bkt.shape, F32)
    for b in range(NUM_BUCKETS):
        val = jnp.where(bkt == b, rb_ref[b, h], val)
    full = jnp.broadcast_to(val * LOG2E, (t, 2 * t))
    skew = pltpu.roll(full, 0, 1, stride=1, stride_axis=0)
    o_ref[0, 0] = skew[:, :t]


def _bias_tiles(rel_bias, t):
    m = jnp.arange(2 * t, dtype=jnp.int32)
    q_minus_k = jnp.where(m < t, m, m - 2 * t)
    half = N_BIAS_TILES // 2
    tile_off = jnp.arange(-half, half + 1, dtype=jnp.int32)
    bkt = _t5_bucket(tile_off[:, None] * t - q_minus_k[None, :]).reshape(N_BIAS_TILES, 1, 2 * t)
    return pl.pallas_call(
        _bias_tile_kernel,
        out_shape=jax.ShapeDtypeStruct((N_HEADS, N_BIAS_TILES, t, t), F32),
        grid=(N_HEADS, N_BIAS_TILES),
        in_specs=[
            pl.BlockSpec(memory_space=pltpu.SMEM),
            pl.BlockSpec((1, 1, 2 * t), lambda h, k: (k, 0, 0)),
        ],
        out_specs=pl.BlockSpec((1, 1, t, t), lambda h, k: (h, k, 0, 0)),
        compiler_params=pltpu.CompilerParams(dimension_semantics=("parallel", "parallel")),
        name="rel_bias_tiles",
    )(rel_bias, bkt)


def _attn_kernel(q_ref, k_ref, vt_ref, bias_ref, lam_ref, sg_ref, o_ref, *, lambda_init):
    i = pl.program_id(2)
    half = N_BIAS_TILES // 2
    q = q_ref[0]
    t_blk = q.shape[0]
    nk = k_ref.shape[1] // t_blk
    m = [None, None]
    l = [None, None]
    acc = [None, None]

    def scores(c):
        j, t = divmod(c, 2)
        hs = slice(t * HEAD_DIM, (t + 1) * HEAD_DIM)
        bias = bias_ref[0, jnp.clip(j - i, -half, half) + half]
        k = k_ref[0, j * t_blk:(j + 1) * t_blk, hs]
        return lax.dot_general(k, q[:, hs], _NT_DIMS, preferred_element_type=F32) + bias

    def weighted_values(c, p, alpha):
        j, t = divmod(c, 2)
        pv = _dot(vt_ref[0, :, j * t_blk:(j + 1) * t_blk], p)
        acc[t] = pv if alpha is None else alpha * acc[t] + pv

    s_next = scores(0)
    pending = None
    for c in range(2 * nk):
        j, t = divmod(c, 2)
        s = s_next
        if c + 1 < 2 * nk:
            s_next = scores(c + 1)
        m_cur = jnp.max(s, axis=0, keepdims=True)
        if j == 0:
            alpha = None
            m[t] = m_cur
            p = jnp.exp2(s - m_cur)
            l[t] = jnp.sum(p, axis=0, keepdims=True)
        else:
            m_new = jnp.maximum(m[t], m_cur)
            alpha = jnp.exp2(m[t] - m_new)
            p = jnp.exp2(s - m_new)
            l[t] = alpha * l[t] + jnp.sum(p, axis=0, keepdims=True)
            m[t] = m_new
        if pending is not None:
            weighted_values(*pending)
        pending = (c, p.astype(BF16), alpha)
    weighted_values(*pending)

    lv = lam_ref[...]
    lam = (jnp.exp(jnp.sum(lv[0:1, :] * lv[1:2, :], axis=-1, keepdims=True))
           - jnp.exp(jnp.sum(lv[2:3, :] * lv[3:4, :], axis=-1, keepdims=True))
           + lambda_init)
    o = acc[0] / l[0] - lam * (acc[1] / l[1])
    ms = jnp.mean(o * o, axis=0, keepdims=True)
    o = (o * lax.rsqrt(ms + EPS)) * sg_ref[...] * (1.0 - lambda_init)
    o_ref[0] = o.T.astype(BF16)


def _attention(q, k, vt, bias_tiles, lam_vecs, subln_g, lambda_init):
    b, s, d = q.shape
    t = ATTN_TILE
    hw = 2 * HEAD_DIM
    qo_spec = pl.BlockSpec((1, t, hw), lambda h, bi, i: (bi, i, h))
    return pl.pallas_call(
        functools.partial(_attn_kernel, lambda_init=lambda_init),
        out_shape=jax.ShapeDtypeStruct((b, s, d), BF16),
        grid=(N_HEADS, b, s // t),
        in_specs=[
            qo_spec,
            pl.BlockSpec((1, s, hw), lambda h, bi, i: (bi, 0, h)),
            pl.BlockSpec((1, hw, s), lambda h, bi, i: (bi, h, 0)),
            pl.BlockSpec((1, N_BIAS_TILES, t, t), lambda h, bi, i: (h, 0, 0, 0)),
            pl.BlockSpec((4, HEAD_DIM), lambda h, bi, i: (0, 0)),
            pl.BlockSpec((hw, 1), lambda h, bi, i: (0, 0)),
        ],
        out_specs=qo_spec,
        compiler_params=pltpu.CompilerParams(
            dimension_semantics=("parallel", "parallel", "parallel"),
            vmem_limit_bytes=_vmem_limit(2 * N_BIAS_TILES * t * t * 4 + 8 * s * hw + (24 << 20))),
        name="diff_attention",
    )(q, k, vt, bias_tiles, lam_vecs, subln_g)


def _oproj_kernel(x_ref, a_ref, g_ref, w_ref, o_ref):
    y = _dot(a_ref[...], w_ref[...])
    o_ref[...] = x_ref[...] + _rms(y, g_ref[1:2, :])


def _out_proj(x2, a2, g, w_o):
    n_tok, d = x2.shape
    tm = TOKEN_TILE
    tile = pl.BlockSpec((tm, d), lambda i: (i, 0))
    return pl.pallas_call(
        _oproj_kernel,
        out_shape=jax.ShapeDtypeStruct(x2.shape, F32),
        grid=(n_tok // tm,),
        in_specs=[tile, tile, _const_spec((4, d)), _const_spec((d, d))],
        out_specs=tile,
        compiler_params=pltpu.CompilerParams(
            dimension_semantics=("parallel",),
            vmem_limit_bytes=_vmem_limit(d * d * 2 + (24 << 20))),
        name="attn_out_proj",
    )(x2, a2, g, w_o)


def kernel(x, norm_g, conv_w_in, conv_w, conv_w_out, pool_w, pool_scale, attn_w_qkv, attn_w_o,
           lambda_q1, lambda_k1, lambda_q2, lambda_k2, attn_subln_g, rel_bias,
           ffn_w_gate, ffn_w_up, ffn_w_down):
    b, s, d = x.shape
    bias_tiles = None
    ia = ib = ic = 0
    for i in range(DEPTH):
        g = norm_g[i]
        kind = i % N_MIXERS
        if kind == 0:
            x = _conv_mixer(x, g, conv_w_in[ia].astype(BF16), conv_w[ia],
                            conv_w_out[ia].astype(BF16))
            ia += 1
        elif kind == 1:
            x = _pool_mixer(x, g, pool_w[ib].astype(BF16), pool_scale[ib].reshape(1, d))
            ib += 1
        else:
            if bias_tiles is None:
                bias_tiles = _bias_tiles(rel_bias, ATTN_TILE)
            w_qkv = attn_w_qkv[ic]
            q, k, vt = _qkv_proj(x, g, w_qkv[:, :2 * d].astype(BF16),
                                 w_qkv[:, 2 * d:].T.astype(BF16))
            lam_vecs = jnp.stack([lambda_q1[ic], lambda_k1[ic], lambda_q2[ic], lambda_k2[ic]])
            a = _attention(q, k, vt, bias_tiles, lam_vecs,
                           attn_subln_g[ic].reshape(2 * HEAD_DIM, 1), _lambda_init(i))
            x = _out_proj(x.reshape(b * s, d), a.reshape(b * s, d), g,
                          attn_w_o[ic].astype(BF16)).reshape(b, s, d)
            ic += 1
        x = _ffn(x.reshape(b * s, d), g, ffn_w_gate[i].astype(BF16), ffn_w_up[i].astype(BF16),
                 ffn_w_down[i].astype(BF16)).reshape(b, s, d)
    return x
```

```python
import functools
import math

import jax
import jax.numpy as jnp
from jax import lax
from jax.experimental import pallas as pl
from jax.experimental.pallas import tpu as pltpu

D_MODEL = 1024
DEPTH = 4
N_MIXERS = 3
CONV_WIDTH = 3
POOL_WINDOWS = (2, 4, 8, 16)
POOL_GROUP_CH = D_MODEL // len(POOL_WINDOWS)
HEAD_DIM = 128
N_HEADS = D_MODEL // (2 * HEAD_DIM)
NUM_BUCKETS = 32
MAX_DISTANCE = 128
D_FF = ((8 * D_MODEL // 3 + 255) // 256) * 256
EPS = 1e-6

F32 = jnp.float32
BF16 = jnp.bfloat16

SUBLANES = 8
LANES = 128
MXU_DIM = 256
VMEM_BYTES = 64 * 1024 * 1024

TOKEN_TILE = 512
POOL_TILE = 256
ATTN_TILE = 512
FF_CHUNK = MXU_DIM
CONV_CHUNK = MXU_DIM
HALO_PAD = LANES


def _lambda_init(layer_idx):
    return 0.8 - 0.6 * math.exp(-0.3 * layer_idx)


def _vmem_limit(nbytes):
    return int(min(VMEM_BYTES - (4 << 20), nbytes))


def _rms(x, g):
    ms = jnp.mean(x * x, axis=-1, keepdims=True)
    return (x * lax.rsqrt(ms + EPS)) * g


def _dot(a, b):
    return jnp.dot(a, b, preferred_element_type=F32)


def _const_spec(shape):
    return pl.BlockSpec(shape, lambda *_: (0,) * len(shape), pipeline_mode=pl.Buffered(1))


def _ffn_kernel(x_ref, g_ref, wg_ref, wu_ref, wd_ref, o_ref, h_scr):
    x = x_ref[...]
    xn = _rms(x, g_ref[2:3, :]).astype(BF16)
    for c in range(0, D_FF, FF_CHUNK):
        gate = _dot(xn, wg_ref[:, c:c + FF_CHUNK])
        up = _dot(xn, wu_ref[:, c:c + FF_CHUNK])
        h_scr[:, c:c + FF_CHUNK] = (jax.nn.silu(gate) * up).astype(BF16)
    f = _dot(h_scr[...], wd_ref[...])
    o_ref[...] = x + _rms(f, g_ref[3:4, :])


def _ffn(x2, g, wg, wu, wd):
    n_tok = x2.shape[0]
    tm = TOKEN_TILE
    weights = 3 * D_MODEL * D_FF * 2
    tiles = 4 * tm * D_MODEL * 4 + tm * D_FF * 2
    return pl.pallas_call(
        _ffn_kernel,
        out_shape=jax.ShapeDtypeStruct(x2.shape, F32),
        grid=(n_tok // tm,),
        in_specs=[
            pl.BlockSpec((tm, D_MODEL), lambda i: (i, 0)),
            _const_spec((4, D_MODEL)),
            _const_spec((D_MODEL, D_FF)),
            _const_spec((D_MODEL, D_FF)),
            _const_spec((D_FF, D_MODEL)),
        ],
        out_specs=pl.BlockSpec((tm, D_MODEL), lambda i: (i, 0)),
        scratch_shapes=[pltpu.VMEM((tm, D_FF), BF16)],
        compiler_params=pltpu.CompilerParams(
            dimension_semantics=("parallel",),
            vmem_limit_bytes=_vmem_limit(weights + tiles + (16 << 20))),
        name="ffn",
    )(x2, g, wg, wu, wd)


def _conv_kernel(xp_ref, x_ref, xn_ref, g_ref, win_ref, cw_ref, wout_ref, o_ref, v_scr):
    i = pl.program_id(1)
    last = pl.num_programs(1) - 1
    tm = x_ref.shape[1]
    d = D_MODEL
    x = x_ref[0]
    g0 = g_ref[0:1, :]
    hn = _rms(x, g0).astype(BF16)
    halo = jnp.concatenate([xp_ref[0], xn_ref[0]], axis=0)
    hh = _rms(halo, g0).astype(BF16)
    row = lax.broadcasted_iota(jnp.int32, (tm, 1), 0)
    for c in range(0, d, CONV_CHUNK):
        cs = slice(c, c + CONV_CHUNK)
        gate_b = _dot(hn, win_ref[:, c:c + CONV_CHUNK])
        u = (_dot(hn, win_ref[:, d + c:d + c + CONV_CHUNK])
             * _dot(hn, win_ref[:, 2 * d + c:2 * d + c + CONV_CHUNK]))
        uh = (_dot(hh, win_ref[:, d + c:d + c + CONV_CHUNK])
              * _dot(hh, win_ref[:, 2 * d + c:2 * d + c + CONV_CHUNK]))
        u_prev = jnp.where(i > 0, uh[SUBLANES - 1:SUBLANES, :], 0.0)
        u_next = jnp.where(i < last, uh[SUBLANES:SUBLANES + 1, :], 0.0)
        u_dn = jnp.where(row == 0, u_prev, pltpu.roll(u, 1, axis=0))
        u_up = jnp.where(row == tm - 1, u_next, pltpu.roll(u, tm - 1, axis=0))
        conv = cw_ref[0:1, cs] * u_dn + cw_ref[1:2, cs] * u + cw_ref[2:3, cs] * u_up
        v_scr[:, cs] = (gate_b * conv).astype(BF16)
    y = _dot(v_scr[...], wout_ref[...])
    o_ref[0] = x + _rms(y, g_ref[1:2, :])


def _conv_mixer(x, g, w_in, w_conv, w_out):
    b, s, d = x.shape
    tm = TOKEN_TILE
    nt = s // tm
    hb = tm // SUBLANES
    weights = (3 * d * d + d * d) * 2
    tiles = 4 * tm * d * 4 + tm * d * 2
    return pl.pallas_call(
        _conv_kernel,
        out_shape=jax.ShapeDtypeStruct(x.shape, F32),
        grid=(b, nt),
        in_specs=[
            pl.BlockSpec((1, SUBLANES, d), lambda bi, i: (bi, jnp.maximum(i * hb - 1, 0), 0)),
            pl.BlockSpec((1, tm, d), lambda bi, i: (bi, i, 0)),
            pl.BlockSpec((1, SUBLANES, d),
                         lambda bi, i: (bi, jnp.minimum((i + 1) * hb, s // SUBLANES - 1), 0)),
            _const_spec((4, d)),
            _const_spec((d, 3 * d)),
            _const_spec((CONV_WIDTH, d)),
            _const_spec((d, d)),
        ],
        out_specs=pl.BlockSpec((1, tm, d), lambda bi, i: (bi, i, 0)),
        scratch_shapes=[pltpu.VMEM((tm, d), BF16)],
        compiler_params=pltpu.CompilerParams(
            dimension_semantics=("parallel", "parallel"),
            vmem_limit_bytes=_vmem_limit(weights + tiles + (16 << 20))),
        name="conv_mixer",
    )(x, x, x, g, w_in, w_conv, w_out)


def _pool_kernel(xp_ref, x_ref, xn_ref, g_ref, wp_ref, ps_ref, o_ref, *, seq_len):
    i = pl.program_id(1)
    last = pl.num_programs(1) - 1
    tm = x_ref.shape[1]
    te = tm + HALO_PAD
    x = x_ref[0]
    g0 = g_ref[0:1, :]
    ext = jnp.concatenate(
        [xp_ref[0], x, xn_ref[0], jnp.zeros((HALO_PAD - 2 * SUBLANES, D_MODEL), F32)], axis=0)
    erow = lax.broadcasted_iota(jnp.int32, (te, 1), 0)
    inside = ((erow >= SUBLANES) | (i > 0)) & ((erow < tm + SUBLANES) | (i < last))
    hn_ext = jnp.where(inside, _rms(ext, g0), 0.0)
    hn_ext_b = hn_ext.astype(BF16)
    hn = hn_ext[SUBLANES:SUBLANES + tm, :]
    rel = (lax.broadcasted_iota(jnp.int32, (tm, te), 1) - SUBLANES
           - lax.broadcasted_iota(jnp.int32, (tm, te), 0))
    t = i * tm + lax.broadcasted_iota(jnp.int32, (tm, 1), 0)
    outs = []
    for gi, w in enumerate(POOL_WINDOWS):
        lo_off, hi_off = -(w // 2), w - 1 - w // 2
        cs = slice(gi * POOL_GROUP_CH, (gi + 1) * POOL_GROUP_CH)
        band = jnp.where((rel >= lo_off) & (rel <= hi_off), 1.0, 0.0).astype(BF16)
        win_sum = _dot(band, hn_ext_b[:, cs])
        cnt = (jnp.minimum(t + hi_off, seq_len - 1) - jnp.maximum(t + lo_off, 0) + 1).astype(F32)
        p = win_sum / cnt - hn[:, cs]
        outs.append(_dot(p.astype(BF16), wp_ref[gi]))
    y = jnp.concatenate(outs, axis=1) * ps_ref[...]
    o_ref[0] = x + _rms(y, g_ref[1:2, :])


def _pool_mixer(x, g, w_pool, scale):
    b, s, d = x.shape
    tm = POOL_TILE
    nt = s // tm
    hb = tm // SUBLANES
    ng = len(POOL_WINDOWS)
    return pl.pallas_call(
        functools.partial(_pool_kernel, seq_len=s),
        out_shape=jax.ShapeDtypeStruct(x.shape, F32),
        grid=(b, nt),
        in_specs=[
            pl.BlockSpec((1, SUBLANES, d), lambda bi, i: (bi, jnp.maximum(i * hb - 1, 0), 0)),
            pl.BlockSpec((1, tm, d), lambda bi, i: (bi, i, 0)),
            pl.BlockSpec((1, SUBLANES, d),
                         lambda bi, i: (bi, jnp.minimum((i + 1) * hb, s // SUBLANES - 1), 0)),
            _const_spec((4, d)),
            _const_spec((ng, POOL_GROUP_CH, POOL_GROUP_CH)),
            _const_spec((1, d)),
        ],
        out_specs=pl.BlockSpec((1, tm, d), lambda bi, i: (bi, i, 0)),
        compiler_params=pltpu.CompilerParams(
            dimension_semantics=("parallel", "parallel"),
            vmem_limit_bytes=_vmem_limit(32 << 20)),
        name="pool_mixer",
    )(x, x, x, g, w_pool, scale)


_NT_DIMS = (((1,), (1,)), ((), ()))
LOG2E = math.log2(math.e)


ONES_ROWS = 16
VT_ROWS = 2 * HEAD_DIM + ONES_ROWS


def _qkv_kernel(x_ref, g_ref, wqk_ref, wvt_ref, q_ref, k_ref, vt_ref):
    d = D_MODEL
    hw = 2 * HEAD_DIM
    hn = _rms(x_ref[0], g_ref[0:1, :]).astype(BF16)
    q_ref[0] = (_dot(hn, wqk_ref[:, :d]) * (HEAD_DIM ** -0.5 * LOG2E)).astype(BF16)
    k_ref[0] = _dot(hn, wqk_ref[:, d:]).astype(BF16)
    vt = lax.dot_general(wvt_ref[...], hn, _NT_DIMS, preferred_element_type=F32).astype(BF16)
    for h in range(N_HEADS):
        vt_ref[0, h, :hw, :] = vt[h * hw:(h + 1) * hw, :]
        vt_ref[0, h, hw:, :] = jnp.ones((ONES_ROWS, vt.shape[1]), BF16)


def _qkv_proj(x, g, w_qk, w_vt):
    b, s, d = x.shape
    tm = TOKEN_TILE
    tile = pl.BlockSpec((1, tm, d), lambda bi, i: (bi, i, 0))
    return pl.pallas_call(
        _qkv_kernel,
        out_shape=(jax.ShapeDtypeStruct((b, s, d), BF16), jax.ShapeDtypeStruct((b, s, d), BF16),
                   jax.ShapeDtypeStruct((b, N_HEADS, VT_ROWS, s), BF16)),
        grid=(b, s // tm),
        in_specs=[tile, _const_spec((4, d)), _const_spec((d, 2 * d)), _const_spec((d, d))],
        out_specs=(tile, tile,
                   pl.BlockSpec((1, N_HEADS, VT_ROWS, tm), lambda bi, i: (bi, 0, 0, i))),
        compiler_params=pltpu.CompilerParams(
            dimension_semantics=("parallel", "parallel"),
            vmem_limit_bytes=_vmem_limit(3 * d * d * 2 + (24 << 20))),
        name="qkv_proj",
    )(x, g, w_qk, w_vt)


def _t5_bucket(rel):
    half = NUM_BUCKETS // 2
    ret = jnp.where(rel > 0, half, 0)
    n = jnp.abs(rel)
    max_exact = half // 2
    nf = jnp.maximum(n, 1).astype(F32)
    large = max_exact + (jnp.log(nf / max_exact) / math.log(MAX_DISTANCE / max_exact)
                         * (half - max_exact)).astype(jnp.int32)
    large = jnp.minimum(large, half - 1)
    return ret + jnp.where(n < max_exact, n, large)


N_BIAS_TILES = 5


def _bias_tile_kernel(rb_ref, bkt_ref, o_ref):
    h = pl.program_id(0)
    t = o_ref.shape[-1]
    bkt = bkt_ref[0]
    val = jnp.zeros(bkt.shape, F32)
    for b in range(NUM_BUCKETS):
        val = jnp.where(bkt == b, rb_ref[b, h], val)
    full = jnp.broadcast_to(val * LOG2E, (t, 2 * t))
    skew = pltpu.roll(full, 0, 1, stride=1, stride_axis=0)
    o_ref[0, 0] = skew[:, :t]


def _bias_tiles(rel_bias, t):
    m = jnp.arange(2 * t, dtype=jnp.int32)
    q_minus_k = jnp.where(m < t, m, m - 2 * t)
    half = N_BIAS_TILES // 2
    tile_off = jnp.arange(-half, half + 1, dtype=jnp.int32)
    bkt = _t5_bucket(tile_off[:, None] * t - q_minus_k[None, :]).reshape(N_BIAS_TILES, 1, 2 * t)
    return pl.pallas_call(
        _bias_tile_kernel,
        out_shape=jax.ShapeDtypeStruct((N_HEADS, N_BIAS_TILES, t, t), F32),
        grid=(N_HEADS, N_BIAS_TILES),
        in_specs=[
            pl.BlockSpec(memory_space=pltpu.SMEM),
            pl.BlockSpec((1, 1, 2 * t), lambda h, k: (k, 0, 0)),
        ],
        out_specs=pl.BlockSpec((1, 1, t, t), lambda h, k: (h, k, 0, 0)),
        compiler_params=pltpu.CompilerParams(dimension_semantics=("parallel", "parallel")),
        name="rel_bias_tiles",
    )(rel_bias, bkt)


def _attn_kernel(q_ref, k_ref, vt_ref, bias_ref, lam_ref, sg_ref, o_ref, *, lambda_init):
    i = pl.program_id(2)
    half = N_BIAS_TILES // 2
    q = q_ref[0]
    t_blk = q.shape[0]
    nk = k_ref.shape[1] // t_blk
    hw = 2 * HEAD_DIM
    m = [None, None]
    acc = [None, None]

    def scores(c):
        j, t = divmod(c, 2)
        hs = slice(t * HEAD_DIM, (t + 1) * HEAD_DIM)
        bias = bias_ref[0, jnp.clip(j - i, -half, half) + half]
        k = k_ref[0, j * t_blk:(j + 1) * t_blk, hs]
        return lax.dot_general(k, q[:, hs], _NT_DIMS, preferred_element_type=F32) + bias

    def weighted_values(c, p, alpha):
        j, t = divmod(c, 2)
        pv = _dot(vt_ref[0, 0, :, j * t_blk:(j + 1) * t_blk], p)
        acc[t] = pv if alpha is None else alpha * acc[t] + pv

    s_next = scores(0)
    pending = None
    for c in range(2 * nk):
        j, t = divmod(c, 2)
        s = s_next
        if c + 1 < 2 * nk:
            s_next = scores(c + 1)
        m_cur = jnp.max(s, axis=0, keepdims=True)
        if j == 0:
            alpha = None
            m[t] = m_cur
            p = jnp.exp2(s - m_cur)
        else:
            m_new = jnp.maximum(m[t], m_cur)
            alpha = jnp.exp2(m[t] - m_new)
            p = jnp.exp2(s - m_new)
            m[t] = m_new
        if pending is not None:
            weighted_values(*pending)
        pending = (c, p.astype(BF16), alpha)
    weighted_values(*pending)

    lv = lam_ref[...]
    lam = (jnp.exp(jnp.sum(lv[0:1, :] * lv[1:2, :], axis=-1, keepdims=True))
           - jnp.exp(jnp.sum(lv[2:3, :] * lv[3:4, :], axis=-1, keepdims=True))
           + lambda_init)
    w1 = 1.0 / acc[0][hw:hw + 1, :]
    w2 = lam / acc[1][hw:hw + 1, :]
    o = acc[0][:hw, :] * w1 - acc[1][:hw, :] * w2
    ms = jnp.mean(o * o, axis=0, keepdims=True)
    o = (o * lax.rsqrt(ms + EPS)).T
    o_ref[0] = (o * (sg_ref[...] * (1.0 - lambda_init))).astype(BF16)


def _attention(q, k, vt, bias_tiles, lam_vecs, subln_g, lambda_init):
    b, s, d = q.shape
    t = ATTN_TILE
    hw = 2 * HEAD_DIM
    qo_spec = pl.BlockSpec((1, t, hw), lambda h, bi, i: (bi, i, h))
    return pl.pallas_call(
        functools.partial(_attn_kernel, lambda_init=lambda_init),
        out_shape=jax.ShapeDtypeStruct((b, s, d), BF16),
        grid=(N_HEADS, b, s // t),
        in_specs=[
            qo_spec,
            pl.BlockSpec((1, s, hw), lambda h, bi, i: (bi, 0, h)),
            pl.BlockSpec((1, 1, VT_ROWS, s), lambda h, bi, i: (bi, h, 0, 0)),
            pl.BlockSpec((1, N_BIAS_TILES, t, t), lambda h, bi, i: (h, 0, 0, 0)),
            pl.BlockSpec((4, HEAD_DIM), lambda h, bi, i: (0, 0)),
            pl.BlockSpec((1, hw), lambda h, bi, i: (0, 0)),
        ],
        out_specs=qo_spec,
        compiler_params=pltpu.CompilerParams(
            dimension_semantics=("parallel", "parallel", "parallel"),
            vmem_limit_bytes=_vmem_limit(2 * N_BIAS_TILES * t * t * 4 + 8 * s * hw + (24 << 20))),
        name="diff_attention",
    )(q, k, vt, bias_tiles, lam_vecs, subln_g)


def _oproj_kernel(x_ref, a_ref, g_ref, w_ref, o_ref):
    y = _dot(a_ref[...], w_ref[...])
    o_ref[...] = x_ref[...] + _rms(y, g_ref[1:2, :])


def _out_proj(x2, a2, g, w_o):
    n_tok, d = x2.shape
    tm = TOKEN_TILE
    tile = pl.BlockSpec((tm, d), lambda i: (i, 0))
    return pl.pallas_call(
        _oproj_kernel,
        out_shape=jax.ShapeDtypeStruct(x2.shape, F32),
        grid=(n_tok // tm,),
        in_specs=[tile, tile, _const_spec((4, d)), _const_spec((d, d))],
        out_specs=tile,
        compiler_params=pltpu.CompilerParams(
            dimension_semantics=("parallel",),
            vmem_limit_bytes=_vmem_limit(d * d * 2 + (24 << 20))),
        name="attn_out_proj",
    )(x2, a2, g, w_o)


def kernel(x, norm_g, conv_w_in, conv_w, conv_w_out, pool_w, pool_scale, attn_w_qkv, attn_w_o,
           lambda_q1, lambda_k1, lambda_q2, lambda_k2, attn_subln_g, rel_bias,
           ffn_w_gate, ffn_w_up, ffn_w_down):
    b, s, d = x.shape
    bias_tiles = None
    ia = ib = ic = 0
    for i in range(DEPTH):
        g = norm_g[i]
        kind = i % N_MIXERS
        if kind == 0:
            x = _conv_mixer(x, g, conv_w_in[ia].astype(BF16), conv_w[ia],
                            conv_w_out[ia].astype(BF16))
            ia += 1
        elif kind == 1:
            x = _pool_mixer(x, g, pool_w[ib].astype(BF16), pool_scale[ib].reshape(1, d))
            ib += 1
        else:
            if bias_tiles is None:
                bias_tiles = _bias_tiles(rel_bias, ATTN_TILE)
            w_qkv = attn_w_qkv[ic]
            q, k, vt = _qkv_proj(x, g, w_qkv[:, :2 * d].astype(BF16),
                                 w_qkv[:, 2 * d:].T.astype(BF16))
            lam_vecs = jnp.stack([lambda_q1[ic], lambda_k1[ic], lambda_q2[ic], lambda_k2[ic]])
            a = _attention(q, k, vt, bias_tiles, lam_vecs,
                           attn_subln_g[ic].reshape(1, 2 * HEAD_DIM), _lambda_init(i))
            x = _out_proj(x.reshape(b * s, d), a.reshape(b * s, d), g,
                          attn_w_o[ic].astype(BF16)).reshape(b, s, d)
            ic += 1
        x = _ffn(x.reshape(b * s, d), g, ffn_w_gate[i].astype(BF16), ffn_w_up[i].astype(BF16),
                 ffn_w_down[i].astype(BF16)).reshape(b, s, d)
    return x
```

```python
import functools
import math

import jax
import jax.numpy as jnp
from jax import lax
from jax.experimental import pallas as pl
from jax.experimental.pallas import tpu as pltpu

D_MODEL = 1024
DEPTH = 4
N_MIXERS = 3
CONV_WIDTH = 3
POOL_WINDOWS = (2, 4, 8, 16)
POOL_GROUP_CH = D_MODEL // len(POOL_WINDOWS)
HEAD_DIM = 128
N_HEADS = D_MODEL // (2 * HEAD_DIM)
NUM_BUCKETS = 32
MAX_DISTANCE = 128
D_FF = ((8 * D_MODEL // 3 + 255) // 256) * 256
EPS = 1e-6

F32 = jnp.float32
BF16 = jnp.bfloat16

SUBLANES = 8
LANES = 128
MXU_DIM = 256
VMEM_BYTES = 64 * 1024 * 1024

TOKEN_TILE = 512
POOL_TILE = 256
ATTN_TILE = 512
ATTN_KEY_TILES = 2
FF_CHUNK = MXU_DIM
CONV_CHUNK = MXU_DIM
HALO_PAD = LANES


def _lambda_init(layer_idx):
    return 0.8 - 0.6 * math.exp(-0.3 * layer_idx)


def _vmem_limit(nbytes):
    return int(min(VMEM_BYTES - (4 << 20), nbytes))


def _rms(x, g):
    ms = jnp.mean(x * x, axis=-1, keepdims=True)
    return (x * lax.rsqrt(ms + EPS)) * g


def _dot(a, b):
    return jnp.dot(a, b, preferred_element_type=F32)


def _const_spec(shape):
    return pl.BlockSpec(shape, lambda *_: (0,) * len(shape), pipeline_mode=pl.Buffered(1))


def _ffn_kernel(x_ref, g_ref, wg_ref, wu_ref, wd_ref, o_ref, h_scr):
    x = x_ref[...]
    xn = _rms(x, g_ref[2:3, :]).astype(BF16)
    for c in range(0, D_FF, FF_CHUNK):
        gate = _dot(xn, wg_ref[:, c:c + FF_CHUNK])
        up = _dot(xn, wu_ref[:, c:c + FF_CHUNK])
        h_scr[:, c:c + FF_CHUNK] = (jax.nn.silu(gate) * up).astype(BF16)
    f = _dot(h_scr[...], wd_ref[...])
    o_ref[...] = x + _rms(f, g_ref[3:4, :])


def _ffn(x2, g, wg, wu, wd):
    n_tok = x2.shape[0]
    tm = TOKEN_TILE
    weights = 3 * D_MODEL * D_FF * 2
    tiles = 4 * tm * D_MODEL * 4 + tm * D_FF * 2
    return pl.pallas_call(
        _ffn_kernel,
        out_shape=jax.ShapeDtypeStruct(x2.shape, F32),
        grid=(n_tok // tm,),
        in_specs=[
            pl.BlockSpec((tm, D_MODEL), lambda i: (i, 0)),
            _const_spec((4, D_MODEL)),
            _const_spec((D_MODEL, D_FF)),
            _const_spec((D_MODEL, D_FF)),
            _const_spec((D_FF, D_MODEL)),
        ],
        out_specs=pl.BlockSpec((tm, D_MODEL), lambda i: (i, 0)),
        scratch_shapes=[pltpu.VMEM((tm, D_FF), BF16)],
        compiler_params=pltpu.CompilerParams(
            dimension_semantics=("parallel",),
            vmem_limit_bytes=_vmem_limit(weights + tiles + (16 << 20))),
        name="ffn",
    )(x2, g, wg, wu, wd)


def _conv_kernel(xp_ref, x_ref, xn_ref, g_ref, win_ref, cw_ref, wout_ref, o_ref, v_scr):
    i = pl.program_id(1)
    last = pl.num_programs(1) - 1
    tm = x_ref.shape[1]
    d = D_MODEL
    x = x_ref[0]
    g0 = g_ref[0:1, :]
    hn = _rms(x, g0).astype(BF16)
    halo = jnp.concatenate([xp_ref[0], xn_ref[0]], axis=0)
    hh = _rms(halo, g0).astype(BF16)
    row = lax.broadcasted_iota(jnp.int32, (tm, 1), 0)
    for c in range(0, d, CONV_CHUNK):
        cs = slice(c, c + CONV_CHUNK)
        gate_b = _dot(hn, win_ref[:, c:c + CONV_CHUNK])
        u = (_dot(hn, win_ref[:, d + c:d + c + CONV_CHUNK])
             * _dot(hn, win_ref[:, 2 * d + c:2 * d + c + CONV_CHUNK]))
        uh = (_dot(hh, win_ref[:, d + c:d + c + CONV_CHUNK])
              * _dot(hh, win_ref[:, 2 * d + c:2 * d + c + CONV_CHUNK]))
        u_prev = jnp.where(i > 0, uh[SUBLANES - 1:SUBLANES, :], 0.0)
        u_next = jnp.where(i < last, uh[SUBLANES:SUBLANES + 1, :], 0.0)
        u_dn = jnp.where(row == 0, u_prev, pltpu.roll(u, 1, axis=0))
        u_up = jnp.where(row == tm - 1, u_next, pltpu.roll(u, tm - 1, axis=0))
        conv = cw_ref[0:1, cs] * u_dn + cw_ref[1:2, cs] * u + cw_ref[2:3, cs] * u_up
        v_scr[:, cs] = (gate_b * conv).astype(BF16)
    y = _dot(v_scr[...], wout_ref[...])
    o_ref[0] = x + _rms(y, g_ref[1:2, :])


def _conv_mixer(x, g, w_in, w_conv, w_out):
    b, s, d = x.shape
    tm = TOKEN_TILE
    nt = s // tm
    hb = tm // SUBLANES
    weights = (3 * d * d + d * d) * 2
    tiles = 4 * tm * d * 4 + tm * d * 2
    return pl.pallas_call(
        _conv_kernel,
        out_shape=jax.ShapeDtypeStruct(x.shape, F32),
        grid=(b, nt),
        in_specs=[
            pl.BlockSpec((1, SUBLANES, d), lambda bi, i: (bi, jnp.maximum(i * hb - 1, 0), 0)),
            pl.BlockSpec((1, tm, d), lambda bi, i: (bi, i, 0)),
            pl.BlockSpec((1, SUBLANES, d),
                         lambda bi, i: (bi, jnp.minimum((i + 1) * hb, s // SUBLANES - 1), 0)),
            _const_spec((4, d)),
            _const_spec((d, 3 * d)),
            _const_spec((CONV_WIDTH, d)),
            _const_spec((d, d)),
        ],
        out_specs=pl.BlockSpec((1, tm, d), lambda bi, i: (bi, i, 0)),
        scratch_shapes=[pltpu.VMEM((tm, d), BF16)],
        compiler_params=pltpu.CompilerParams(
            dimension_semantics=("parallel", "parallel"),
            vmem_limit_bytes=_vmem_limit(weights + tiles + (16 << 20))),
        name="conv_mixer",
    )(x, x, x, g, w_in, w_conv, w_out)


def _pool_kernel(xp_ref, x_ref, xn_ref, g_ref, wp_ref, ps_ref, o_ref, *, seq_len):
    i = pl.program_id(1)
    last = pl.num_programs(1) - 1
    tm = x_ref.shape[1]
    te = tm + HALO_PAD
    x = x_ref[0]
    g0 = g_ref[0:1, :]
    ext = jnp.concatenate(
        [xp_ref[0], x, xn_ref[0], jnp.zeros((HALO_PAD - 2 * SUBLANES, D_MODEL), F32)], axis=0)
    erow = lax.broadcasted_iota(jnp.int32, (te, 1), 0)
    inside = ((erow >= SUBLANES) | (i > 0)) & ((erow < tm + SUBLANES) | (i < last))
    hn_ext = jnp.where(inside, _rms(ext, g0), 0.0)
    hn_ext_b = hn_ext.astype(BF16)
    hn = hn_ext[SUBLANES:SUBLANES + tm, :]
    rel = (lax.broadcasted_iota(jnp.int32, (tm, te), 1) - SUBLANES
           - lax.broadcasted_iota(jnp.int32, (tm, te), 0))
    t = i * tm + lax.broadcasted_iota(jnp.int32, (tm, 1), 0)
    outs = []
    for gi, w in enumerate(POOL_WINDOWS):
        lo_off, hi_off = -(w // 2), w - 1 - w // 2
        cs = slice(gi * POOL_GROUP_CH, (gi + 1) * POOL_GROUP_CH)
        band = jnp.where((rel >= lo_off) & (rel <= hi_off), 1.0, 0.0).astype(BF16)
        win_sum = _dot(band, hn_ext_b[:, cs])
        cnt = (jnp.minimum(t + hi_off, seq_len - 1) - jnp.maximum(t + lo_off, 0) + 1).astype(F32)
        p = win_sum / cnt - hn[:, cs]
        outs.append(_dot(p.astype(BF16), wp_ref[gi]))
    y = jnp.concatenate(outs, axis=1) * ps_ref[...]
    o_ref[0] = x + _rms(y, g_ref[1:2, :])


def _pool_mixer(x, g, w_pool, scale):
    b, s, d = x.shape
    tm = POOL_TILE
    nt = s // tm
    hb = tm // SUBLANES
    ng = len(POOL_WINDOWS)
    return pl.pallas_call(
        functools.partial(_pool_kernel, seq_len=s),
        out_shape=jax.ShapeDtypeStruct(x.shape, F32),
        grid=(b, nt),
        in_specs=[
            pl.BlockSpec((1, SUBLANES, d), lambda bi, i: (bi, jnp.maximum(i * hb - 1, 0), 0)),
            pl.BlockSpec((1, tm, d), lambda bi, i: (bi, i, 0)),
            pl.BlockSpec((1, SUBLANES, d),
                         lambda bi, i: (bi, jnp.minimum((i + 1) * hb, s // SUBLANES - 1), 0)),
            _const_spec((4, d)),
            _const_spec((ng, POOL_GROUP_CH, POOL_GROUP_CH)),
            _const_spec((1, d)),
        ],
        out_specs=pl.BlockSpec((1, tm, d), lambda bi, i: (bi, i, 0)),
        compiler_params=pltpu.CompilerParams(
            dimension_semantics=("parallel", "parallel"),
            vmem_limit_bytes=_vmem_limit(32 << 20)),
        name="pool_mixer",
    )(x, x, x, g, w_pool, scale)


_NT_DIMS = (((1,), (1,)), ((), ()))
LOG2E = math.log2(math.e)


ONES_ROWS = 16
VT_ROWS = 2 * HEAD_DIM + ONES_ROWS


def _qkv_kernel(x_ref, g_ref, wqk_ref, wvt_ref, q_ref, k_ref, vt_ref):
    d = D_MODEL
    hw = 2 * HEAD_DIM
    hn = _rms(x_ref[0], g_ref[0:1, :]).astype(BF16)
    q_ref[0] = (_dot(hn, wqk_ref[:, :d]) * (HEAD_DIM ** -0.5 * LOG2E)).astype(BF16)
    k_ref[0] = _dot(hn, wqk_ref[:, d:]).astype(BF16)
    vt = lax.dot_general(wvt_ref[...], hn, _NT_DIMS, preferred_element_type=F32).astype(BF16)
    for h in range(N_HEADS):
        vt_ref[0, h, :hw, :] = vt[h * hw:(h + 1) * hw, :]
        vt_ref[0, h, hw:, :] = jnp.ones((ONES_ROWS, vt.shape[1]), BF16)


def _qkv_proj(x, g, w_qk, w_vt):
    b, s, d = x.shape
    tm = TOKEN_TILE
    tile = pl.BlockSpec((1, tm, d), lambda bi, i: (bi, i, 0))
    return pl.pallas_call(
        _qkv_kernel,
        out_shape=(jax.ShapeDtypeStruct((b, s, d), BF16), jax.ShapeDtypeStruct((b, s, d), BF16),
                   jax.ShapeDtypeStruct((b, N_HEADS, VT_ROWS, s), BF16)),
        grid=(b, s // tm),
        in_specs=[tile, _const_spec((4, d)), _const_spec((d, 2 * d)), _const_spec((d, d))],
        out_specs=(tile, tile,
                   pl.BlockSpec((1, N_HEADS, VT_ROWS, tm), lambda bi, i: (bi, 0, 0, i))),
        compiler_params=pltpu.CompilerParams(
            dimension_semantics=("parallel", "parallel"),
            vmem_limit_bytes=_vmem_limit(3 * d * d * 2 + (24 << 20))),
        name="qkv_proj",
    )(x, g, w_qk, w_vt)


def _t5_bucket(rel):
    half = NUM_BUCKETS // 2
    ret = jnp.where(rel > 0, half, 0)
    n = jnp.abs(rel)
    max_exact = half // 2
    nf = jnp.maximum(n, 1).astype(F32)
    large = max_exact + (jnp.log(nf / max_exact) / math.log(MAX_DISTANCE / max_exact)
                         * (half - max_exact)).astype(jnp.int32)
    large = jnp.minimum(large, half - 1)
    return ret + jnp.where(n < max_exact, n, large)


N_BIAS_TILES = 5


def _bias_tile_kernel(rb_ref, bkt_ref, o_ref):
    h = pl.program_id(0)
    t = o_ref.shape[-1]
    bkt = bkt_ref[0]
    val = jnp.zeros(bkt.shape, F32)
    for b in range(NUM_BUCKETS):
        val = jnp.where(bkt == b, rb_ref[b, h], val)
    full = jnp.broadcast_to(val * LOG2E, (t, 2 * t))
    skew = pltpu.roll(full, 0, 1, stride=1, stride_axis=0)
    o_ref[0, 0] = skew[:, :t]


def _bias_tiles(rel_bias, t):
    m = jnp.arange(2 * t, dtype=jnp.int32)
    q_minus_k = jnp.where(m < t, m, m - 2 * t)
    half = N_BIAS_TILES // 2
    tile_off = jnp.arange(-half, half + 1, dtype=jnp.int32)
    bkt = _t5_bucket(tile_off[:, None] * t - q_minus_k[None, :]).reshape(N_BIAS_TILES, 1, 2 * t)
    return pl.pallas_call(
        _bias_tile_kernel,
        out_shape=jax.ShapeDtypeStruct((N_HEADS, N_BIAS_TILES, t, t), F32),
        grid=(N_HEADS, N_BIAS_TILES),
        in_specs=[
            pl.BlockSpec(memory_space=pltpu.SMEM),
            pl.BlockSpec((1, 1, 2 * t), lambda h, k: (k, 0, 0)),
        ],
        out_specs=pl.BlockSpec((1, 1, t, t), lambda h, k: (h, k, 0, 0)),
        compiler_params=pltpu.CompilerParams(dimension_semantics=("parallel", "parallel")),
        name="rel_bias_tiles",
    )(rel_bias, bkt)


def _attn_kernel(q_ref, k_ref, vt_ref, bias_ref, lam_ref, sg_ref, o_ref, *, lambda_init):
    i = pl.program_id(2)
    half = N_BIAS_TILES // 2
    q = q_ref[0]
    t_blk = q.shape[0]
    tk = ATTN_KEY_TILES * t_blk
    n_chains = 2 * (k_ref.shape[1] // tk)
    hw = 2 * HEAD_DIM
    m = [None, None]
    acc = [None, None]

    def scores(c):
        j, t = divmod(c, 2)
        hs = slice(t * HEAD_DIM, (t + 1) * HEAD_DIM)
        k = k_ref[0, j * tk:(j + 1) * tk, hs]
        s = lax.dot_general(k, q[:, hs], _NT_DIMS, preferred_element_type=F32)
        parts = []
        for r in range(ATTN_KEY_TILES):
            dist = j * ATTN_KEY_TILES + r - i
            parts.append(s[r * t_blk:(r + 1) * t_blk, :]
                         + bias_ref[0, jnp.clip(dist, -half, half) + half])
        return parts[0] if len(parts) == 1 else jnp.concatenate(parts, axis=0)

    def weighted_values(c, p, alpha):
        j, t = divmod(c, 2)
        pv = _dot(vt_ref[0, 0, :, j * tk:(j + 1) * tk], p)
        acc[t] = pv if alpha is None else alpha * acc[t] + pv

    s_next = scores(0)
    pending = None
    for c in range(n_chains):
        j, t = divmod(c, 2)
        s = s_next
        if c + 1 < n_chains:
            s_next = scores(c + 1)
        m_cur = jnp.max(s, axis=0, keepdims=True)
        if j == 0:
            alpha = None
            m[t] = m_cur
            p = jnp.exp2(s - m_cur)
        else:
            m_new = jnp.maximum(m[t], m_cur)
            alpha = jnp.exp2(m[t] - m_new)
            p = jnp.exp2(s - m_new)
            m[t] = m_new
        if pending is not None:
            weighted_values(*pending)
        pending = (c, p.astype(BF16), alpha)
    weighted_values(*pending)

    lv = lam_ref[...]
    lam = (jnp.exp(jnp.sum(lv[0:1, :] * lv[1:2, :], axis=-1, keepdims=True))
           - jnp.exp(jnp.sum(lv[2:3, :] * lv[3:4, :], axis=-1, keepdims=True))
           + lambda_init)
    w1 = 1.0 / acc[0][hw:hw + 1, :]
    w2 = lam / acc[1][hw:hw + 1, :]
    o = acc[0][:hw, :] * w1 - acc[1][:hw, :] * w2
    ms = jnp.mean(o * o, axis=0, keepdims=True)
    o = (o * lax.rsqrt(ms + EPS)).T
    o_ref[0] = (o * (sg_ref[...] * (1.0 - lambda_init))).astype(BF16)


def _attention(q, k, vt, bias_tiles, lam_vecs, subln_g, lambda_init):
    b, s, d = q.shape
    t = ATTN_TILE
    hw = 2 * HEAD_DIM
    qo_spec = pl.BlockSpec((1, t, hw), lambda h, bi, i: (bi, i, h))
    return pl.pallas_call(
        functools.partial(_attn_kernel, lambda_init=lambda_init),
        out_shape=jax.ShapeDtypeStruct((b, s, d), BF16),
        grid=(N_HEADS, b, s // t),
        in_specs=[
            qo_spec,
            pl.BlockSpec((1, s, hw), lambda h, bi, i: (bi, 0, h)),
            pl.BlockSpec((1, 1, VT_ROWS, s), lambda h, bi, i: (bi, h, 0, 0)),
            pl.BlockSpec((1, N_BIAS_TILES, t, t), lambda h, bi, i: (h, 0, 0, 0)),
            pl.BlockSpec((4, HEAD_DIM), lambda h, bi, i: (0, 0)),
            pl.BlockSpec((1, hw), lambda h, bi, i: (0, 0)),
        ],
        out_specs=qo_spec,
        compiler_params=pltpu.CompilerParams(
            dimension_semantics=("parallel", "parallel", "parallel"),
            vmem_limit_bytes=_vmem_limit(2 * N_BIAS_TILES * t * t * 4 + 8 * s * hw + (24 << 20))),
        name="diff_attention",
    )(q, k, vt, bias_tiles, lam_vecs, subln_g)


def _oproj_kernel(x_ref, a_ref, g_ref, w_ref, o_ref):
    y = _dot(a_ref[...], w_ref[...])
    o_ref[...] = x_ref[...] + _rms(y, g_ref[1:2, :])


def _out_proj(x2, a2, g, w_o):
    n_tok, d = x2.shape
    tm = TOKEN_TILE
    tile = pl.BlockSpec((tm, d), lambda i: (i, 0))
    return pl.pallas_call(
        _oproj_kernel,
        out_shape=jax.ShapeDtypeStruct(x2.shape, F32),
        grid=(n_tok // tm,),
        in_specs=[tile, tile, _const_spec((4, d)), _const_spec((d, d))],
        out_specs=tile,
        compiler_params=pltpu.CompilerParams(
            dimension_semantics=("parallel",),
            vmem_limit_bytes=_vmem_limit(d * d * 2 + (24 << 20))),
        name="attn_out_proj",
    )(x2, a2, g, w_o)


def kernel(x, norm_g, conv_w_in, conv_w, conv_w_out, pool_w, pool_scale, attn_w_qkv, attn_w_o,
           lambda_q1, lambda_k1, lambda_q2, lambda_k2, attn_subln_g, rel_bias,
           ffn_w_gate, ffn_w_up, ffn_w_down):
    b, s, d = x.shape
    bias_tiles = None
    ia = ib = ic = 0
    for i in range(DEPTH):
        g = norm_g[i]
        kind = i % N_MIXERS
        if kind == 0:
            x = _conv_mixer(x, g, conv_w_in[ia].astype(BF16), conv_w[ia],
                            conv_w_out[ia].astype(BF16))
            ia += 1
        elif kind == 1:
            x = _pool_mixer(x, g, pool_w[ib].astype(BF16), pool_scale[ib].reshape(1, d))
            ib += 1
        else:
            if bias_tiles is None:
                bias_tiles = _bias_tiles(rel_bias, ATTN_TILE)
            w_qkv = attn_w_qkv[ic]
            q, k, vt = _qkv_proj(x, g, w_qkv[:, :2 * d].astype(BF16),
                                 w_qkv[:, 2 * d:].T.astype(BF16))
            lam_vecs = jnp.stack([lambda_q1[ic], lambda_k1[ic], lambda_q2[ic], lambda_k2[ic]])
            a = _attention(q, k, vt, bias_tiles, lam_vecs,
                           attn_subln_g[ic].reshape(1, 2 * HEAD_DIM), _lambda_init(i))
            x = _out_proj(x.reshape(b * s, d), a.reshape(b * s, d), g,
                          attn_w_o[ic].astype(BF16)).reshape(b, s, d)
            ic += 1
        x = _ffn(x.reshape(b * s, d), g, ffn_w_gate[i].astype(BF16), ffn_w_up[i].astype(BF16),
                 ffn_w_down[i].astype(BF16)).reshape(b, s, d)
    return x
```
